```python
import jax, jax.numpy as jnp
from jax import lax
import numpy as np

D_MODEL = 2048
BATCH = 4
SEQ = 2048
DEPTH = 1

RMS_EPS = 1e-6
LN_EPS = 1e-5

RWKV_WIDTH = D_MODEL // 2
RWKV_HEAD = 64
RWKV_HEADS = RWKV_WIDTH // RWKV_HEAD
LNX_EPS = 64e-5


def _lora_dim(factor, power):
    return max(32, int(round(factor * D_MODEL ** power / 32)) * 32)


DECAY_LORA = _lora_dim(1.8, 0.5)
A_LORA = _lora_dim(1.8, 0.5)
GATE_LORA = _lora_dim(0.6, 0.8)

SGU_WIDTH = D_MODEL // 2
CHUNK = 128
SGU_GROUP_DIM = 128
SGU_GROUPS = SGU_WIDTH // SGU_GROUP_DIM

FFN_HIDDEN = ((8 * D_MODEL // 3 + 255) // 256) * 256

RWKV_COLS = 3 * RWKV_WIDTH + DECAY_LORA + A_LORA + GATE_LORA
SGU_COLS = 2 * SGU_WIDTH
GATE_COLS = 2 * D_MODEL
IN_COLS = RWKV_COLS + SGU_COLS + GATE_COLS
IN_SPLITS = [RWKV_COLS, RWKV_COLS + SGU_COLS, RWKV_COLS + SGU_COLS + D_MODEL]
RWKV_SPLITS = [RWKV_WIDTH, 2 * RWKV_WIDTH, 3 * RWKV_WIDTH,
               3 * RWKV_WIDTH + DECAY_LORA, 3 * RWKV_WIDTH + DECAY_LORA + A_LORA]

kernel_name = "rwkv7_sgu_gated_hybrid_block"


def _rmsnorm(x, g):
    xf = x.astype(jnp.float32)
    y = xf * lax.rsqrt(jnp.mean(xf * xf, axis=-1, keepdims=True) + RMS_EPS)
    return (y * g.astype(jnp.float32)).astype(x.dtype)


def _token_shift(p, mu):
    prev = jnp.pad(p, ((0, 0), (1, 0), (0, 0)))[:, :-1]
    return p + (prev - p) * mu


def _wkv7(r, decay, k, v, a, b):
    Bb, T, H, N = r.shape

    def step(S, inp):
        r_t, w_t, k_t, v_t, a_t, b_t = inp
        sa = jnp.einsum('bhij,bhj->bhi', S, a_t)
        S = (S * w_t[:, :, None, :] + sa[..., None] * b_t[:, :, None, :]
             + v_t[..., None] * k_t[:, :, None, :])
        y_t = jnp.einsum('bhij,bhj->bhi', S, r_t)
        return S, y_t

    xs = tuple(jnp.moveaxis(t, 1, 0) for t in (r, decay, k, v, a, b))
    S0 = jnp.zeros((Bb, H, N, N), jnp.float32)
    _, y = lax.scan(step, S0, xs)
    return jnp.moveaxis(y, 0, 1)


def _rwkv7_time_mix(p, mu, w0, w_lora_up, a0, a_lora_up, g_lora_up, k_k, k_a, r_k, lnx_g, lnx_b):
    Bb, T, _ = p.shape
    H, N = RWKV_HEADS, RWKV_HEAD
    f32 = jnp.float32
    p = _token_shift(p, mu)
    r, k, v, xw, xa, xg = jnp.split(p, RWKV_SPLITS, axis=-1)
    w_log = -jax.nn.softplus(-(w0 + jnp.tanh(xw) @ w_lora_up)) - 0.5
    decay = jnp.exp(-jnp.exp(w_log.astype(f32)))
    a = jax.nn.sigmoid(a0 + xa @ a_lora_up)
    g = jax.nn.sigmoid(xg) @ g_lora_up

    heads = lambda t: t.astype(f32).reshape(Bb, T, H, N)
    kk = heads(k * k_k)
    kk = kk / jnp.maximum(jnp.sqrt(jnp.sum(kk * kk, axis=-1, keepdims=True)), 1e-12)
    k = k * (1.0 + (a - 1.0) * k_a)
    r_h, k_h, v_h, a_h, w_h = heads(r), heads(k), heads(v), heads(a), heads(decay)

    y = _wkv7(r_h, w_h, k_h, v_h, -kk, kk * a_h)
    mean = jnp.mean(y, axis=-1, keepdims=True)
    var = jnp.mean(jnp.square(y - mean), axis=-1, keepdims=True)
    y = ((y - mean) * lax.rsqrt(var + LNX_EPS)).reshape(Bb, T, RWKV_WIDTH)
    y = y * lnx_g.astype(f32) + lnx_b.astype(f32)
    bonus = jnp.sum(r_h * k_h * r_k.astype(f32), axis=-1, keepdims=True) * v_h
    y = y + bonus.reshape(Bb, T, RWKV_WIDTH)
    return (y * g.astype(f32)).astype(p.dtype)


def _chunked_sgu(z, ln_g, ln_b, w_s, b_s):
    Bb, T, _ = z.shape
    z = jax.nn.gelu(z, approximate=False)
    u, v = jnp.split(z, 2, axis=-1)
    vf = v.astype(jnp.float32)
    mean = jnp.mean(vf, axis=-1, keepdims=True)
    var = jnp.mean(jnp.square(vf - mean), axis=-1, keepdims=True)
    v = (((vf - mean) * lax.rsqrt(var + LN_EPS)) * ln_g.astype(jnp.float32)
         + ln_b.astype(jnp.float32)).astype(z.dtype)
    vc = v.reshape(Bb, T // CHUNK, CHUNK, SGU_GROUPS, SGU_GROUP_DIM)
    w_causal = w_s * jnp.tril(jnp.ones((CHUNK, CHUNK), w_s.dtype))
    mixed = jnp.einsum('gts,bnsgc->bntgc', w_causal, vc) + b_s.T[None, None, :, :, None]
    return u * mixed.reshape(Bb, T, SGU_WIDTH)


def setup_inputs(seed: int = 0) -> dict:
    key = jax.random.key(seed)
    ks = iter(jax.random.split(key, 32))
    nrm = lambda shape, scale: jax.random.normal(next(ks), shape, jnp.float32) * scale
    L, RW = DEPTH, RWKV_WIDTH
    return {
        "x": nrm((BATCH, SEQ, D_MODEL), 1.0),
        "norm_mix_g": 1.0 + nrm((L, D_MODEL), 0.02),
        "w_in": nrm((L, D_MODEL, IN_COLS), D_MODEL ** -0.5),
        "shift_mu": jax.random.uniform(next(ks), (L, RWKV_COLS), jnp.float32),
        "w0": -2.0 + nrm((L, RW), 1.0),
        "w_lora_up": nrm((L, DECAY_LORA, RW), DECAY_LORA ** -0.5),
        "a0": nrm((L, RW), 0.5),
        "a_lora_up": nrm((L, A_LORA, RW), A_LORA ** -0.5),
        "g_lora_up": nrm((L, GATE_LORA, RW), GATE_LORA ** -0.5),
        "k_k": 1.0 + nrm((L, RW), 0.1),
        "k_a": 1.0 + nrm((L, RW), 0.1),
        "r_k": nrm((L, RWKV_HEADS, RWKV_HEAD), 0.1),
        "lnx_g": 1.0 + nrm((L, RW), 0.02),
        "lnx_b": nrm((L, RW), 0.02),
        "w_proj_rwkv": nrm((L, RW, D_MODEL), RW ** -0.5),
        "sgu_ln_g": 1.0 + nrm((L, SGU_WIDTH), 0.02),
        "sgu_ln_b": nrm((L, SGU_WIDTH), 0.02),
        "sgu_w": nrm((L, SGU_GROUPS, CHUNK, CHUNK), CHUNK ** -0.5),
        "sgu_b": 1.0 + nrm((L, SGU_GROUPS, CHUNK), 0.02),
        "w_proj_sgu": nrm((L, SGU_WIDTH, D_MODEL), SGU_WIDTH ** -0.5),
        "w_out": nrm((L, D_MODEL, D_MODEL), D_MODEL ** -0.5),
        "norm_ffn_g": 1.0 + nrm((L, D_MODEL), 0.02),
        "w_ffn_gate": nrm((L, D_MODEL, FFN_HIDDEN), D_MODEL ** -0.5),
        "w_ffn_up": nrm((L, D_MODEL, FFN_HIDDEN), D_MODEL ** -0.5),
        "w_ffn_down": nrm((L, FFN_HIDDEN, D_MODEL), FFN_HIDDEN ** -0.5),
        "norm_final_g": 1.0 + nrm((D_MODEL,), 0.02),
    }


def reference(x, norm_mix_g, w_in, shift_mu, w0, w_lora_up, a0, a_lora_up, g_lora_up,
              k_k, k_a, r_k, lnx_g, lnx_b, w_proj_rwkv, sgu_ln_g, sgu_ln_b, sgu_w, sgu_b,
              w_proj_sgu, w_out, norm_ffn_g, w_ffn_gate, w_ffn_up, w_ffn_down, norm_final_g):
    h = x
    for l in range(DEPTH):
        n = _rmsnorm(h, norm_mix_g[l])
        proj = n @ w_in[l]
        p_rwkv, z_sgu, gate_a, gate_b = jnp.split(proj, IN_SPLITS, axis=-1)
        y_a = _rwkv7_time_mix(p_rwkv, shift_mu[l], w0[l], w_lora_up[l], a0[l], a_lora_up[l],
                              g_lora_up[l], k_k[l], k_a[l], r_k[l], lnx_g[l], lnx_b[l])
        y_b = _chunked_sgu(z_sgu, sgu_ln_g[l], sgu_ln_b[l], sgu_w[l], sgu_b[l])
        merged = (jax.nn.sigmoid(gate_a) * (y_a @ w_proj_rwkv[l])
                  + jax.nn.sigmoid(gate_b) * (y_b @ w_proj_sgu[l]))
        h = h + merged @ w_out[l]
        n = _rmsnorm(h, norm_ffn_g[l])
        h = h + (jax.nn.silu(n @ w_ffn_gate[l]) * (n @ w_ffn_up[l])) @ w_ffn_down[l]
    return _rmsnorm(h, norm_final_g)
```

```python
import functools

import jax
import jax.numpy as jnp
from jax import lax
from jax.experimental import pallas as pl
from jax.experimental.pallas import tpu as pltpu

F32 = jnp.float32
BF16 = jnp.bfloat16

RMS_EPS = 1e-6
LN_EPS = 1e-5
LNX_EPS = 64e-5

HEAD = 64
WKV_CHUNK = 64
QUAD = 256
HEADS_PER_QUAD = QUAD // HEAD
SGU_CHUNK = 128
SGU_GROUP = 128
LORA_PAD = 128

VMEM_LIMIT = 56 * 1024 * 1024


def _mm(a, b):
    return jnp.dot(a, b, preferred_element_type=F32)


def _mm_nt(a, b):
    return lax.dot_general(a, b, (((1,), (1,)), ((), ())), preferred_element_type=F32)


def _mm_tn(a, b):
    return lax.dot_general(a, b, (((0,), (0,)), ((), ())), preferred_element_type=F32)


def _rmsnorm(x, g):
    return x * lax.rsqrt(jnp.mean(x * x, axis=-1, keepdims=True) + RMS_EPS) * g


def _norm_matmul_kernel(x_ref, g_ref, w_ref, o_ref, n_ref, *, epilogue):
    @pl.when(pl.program_id(1) == 0)
    def _():
        n_ref[...] = _rmsnorm(x_ref[...], g_ref[...]).astype(BF16)

    y = _mm(n_ref[...], w_ref[...])
    if epilogue == "gelu":
        y = 0.5 * y * (1.0 + lax.erf(y * (2.0 ** -0.5)))
    elif epilogue == "sigmoid":
        y = jax.nn.sigmoid(y)
    o_ref[...] = y.astype(o_ref.dtype)


def _norm_matmul(x, g, w, *, tm, tn, epilogue, out_dtype=F32):
    m, d = x.shape
    n = w.shape[1]
    return pl.pallas_call(
        functools.partial(_norm_matmul_kernel, epilogue=epilogue),
        grid=(m // tm, n // tn),
        in_specs=[
            pl.BlockSpec((tm, d), lambda i, j: (i, 0)),
            pl.BlockSpec((1, d), lambda i, j: (0, 0)),
            pl.BlockSpec((d, tn), lambda i, j: (0, j)),
        ],
        out_specs=pl.BlockSpec((tm, tn), lambda i, j: (i, j)),
        out_shape=jax.ShapeDtypeStruct((m, n), out_dtype),
        scratch_shapes=[pltpu.VMEM((tm, d), BF16)],
        compiler_params=pltpu.CompilerParams(
            dimension_semantics=("parallel", "arbitrary"), vmem_limit_bytes=VMEM_LIMIT),
        name="norm_matmul_" + epilogue,
    )(x, g, w)


def _head_sum(x, ones_bd):
    parts = [_mm(x[:, q * QUAD:(q + 1) * QUAD], ones_bd) for q in range(x.shape[1] // QUAD)]
    return jnp.concatenate(parts, axis=1)


def _rwkv_prep_kernel(p_ref, mu_ref, w0_ref, wup_ref, a0_ref, aup_ref, gup_ref, kk_ref, ka_ref,
                      rk_ref, ah_ref, rh_ref, bh_ref, kh_ref, v_ref, bonus_ref, g_ref, pcb_ref,
                      carry_ref, *, rw):
    tb = p_ref.shape[0]

    @pl.when(pl.program_id(1) == 0)
    def _():
        carry_ref[...] = jnp.zeros_like(carry_ref)

    p = p_ref[...]
    row = lax.broadcasted_iota(jnp.int32, p.shape, 0)
    prev = jnp.where(row == 0, carry_ref[0:1, :], pltpu.roll(p, 1, 0))
    carry_ref[0:1, :] = p[tb - 1:tb, :]
    sh = p + (prev - p) * mu_ref[...]

    r = sh[:, 0:rw]
    k = sh[:, rw:2 * rw]
    v = sh[:, 2 * rw:3 * rw]
    o = 3 * rw
    xw = sh[:, o:o + LORA_PAD]
    xa = sh[:, o + LORA_PAD:o + 2 * LORA_PAD]
    xg = sh[:, o + 2 * LORA_PAD:]

    wl = w0_ref[...] + _mm(jnp.tanh(xw), wup_ref[...])
    z = -wl
    softplus = jnp.maximum(z, 0.0) + jnp.log1p(jnp.exp(-jnp.abs(z)))
    lw = -jnp.exp(-softplus - 0.5)
    a = jax.nn.sigmoid(a0_ref[...] + _mm(xa, aup_ref[...]))
    g_ref[...] = _mm(jax.nn.sigmoid(xg), gup_ref[...])

    qr = lax.broadcasted_iota(jnp.int32, (QUAD, QUAD), 0)
    qc = lax.broadcasted_iota(jnp.int32, (QUAD, QUAD), 1)
    ones_bd = jnp.where(qr // HEAD == qc // HEAD, 1.0, 0.0).astype(F32)

    kk = k * kk_ref[...]
    nrm = jnp.maximum(jnp.sqrt(_head_sum(kk * kk, ones_bd)), 1e-12)
    kk = kk / nrm
    k2 = k * (1.0 + (a - 1.0) * ka_ref[...])
    bonus_ref[...] = _head_sum(r * k2 * rk_ref[...], ones_bd) * v
    v_ref[...] = v

    tr = lax.broadcasted_iota(jnp.int32, (tb, tb), 0)
    tc = lax.broadcasted_iota(jnp.int32, (tb, tb), 1)
    tri = jnp.where((tr // WKV_CHUNK == tc // WKV_CHUNK) & (tc <= tr), 1.0, 0.0).astype(BF16)
    h1 = lw.astype(BF16)
    r1 = lw - h1.astype(F32)
    h2 = r1.astype(BF16)
    h3 = (r1 - h2.astype(F32)).astype(BF16)
    cum = _mm(tri, h1) + _mm(tri, h2) + _mm(tri, h3)

    ec = jnp.exp(cum)
    eci = jnp.exp(-cum)
    ah_ref[...] = -kk * jnp.exp(cum - lw)
    rh_ref[...] = r * ec
    bh_ref[...] = kk * a * eci
    kh_ref[...] = k2 * eci
    for c in range(tb // WKV_CHUNK):
        last = (c + 1) * WKV_CHUNK - 1
        pcb_ref[c] = jnp.broadcast_to(ec[last:last + 1, :], (8, rw))


def _rwkv_prep(p, mu, w0, wup, a0, aup, gup, k_k, k_a, r_k, *, tb):
    bsz, t, cols = p.shape
    rw = w0.shape[1]
    nchunk = t // WKV_CHUNK
    row_spec = lambda w: pl.BlockSpec((1, w), lambda b, i: (0, 0))
    full = lambda a: pl.BlockSpec(a.shape, lambda b, i: (0, 0))
    tok = pl.BlockSpec((None, tb, rw), lambda b, i: (b, i, 0))
    out_tok = jax.ShapeDtypeStruct((bsz, t, rw), F32)
    return pl.pallas_call(
        functools.partial(_rwkv_prep_kernel, rw=rw),
        grid=(bsz, t // tb),
        in_specs=[
            pl.BlockSpec((None, tb, cols), lambda b, i: (b, i, 0)),
            row_spec(cols), row_spec(rw), full(wup), row_spec(rw), full(aup), full(gup),
            row_spec(rw), row_spec(rw), row_spec(rw),
        ],
        out_specs=[tok] * 7 + [
            pl.BlockSpec((None, tb // WKV_CHUNK, 8, rw), lambda b, i: (b, i, 0, 0))],
        out_shape=[out_tok] * 7 + [jax.ShapeDtypeStruct((bsz, nchunk, 8, rw), F32)],
        scratch_shapes=[pltpu.VMEM((8, cols), F32)],
        compiler_params=pltpu.CompilerParams(
            dimension_semantics=("parallel", "arbitrary"), vmem_limit_bytes=VMEM_LIMIT),
        name="rwkv_prep",
    )(p, mu, w0, wup, a0, aup, gup, k_k, k_a, r_k)


def _wkv_kernel(ah_ref, rh_ref, bh_ref, kh_ref, v_ref, pcb_ref, bonus_ref, g_ref, lng_ref,
                lnb_ref, o_ref, state_ref, y_ref):
    tb, rw = o_ref.shape
    c_sz = WKV_CHUNK
    n_quads = rw // QUAD
    rows = HEADS_PER_QUAD * c_sz
    assert rows == QUAD

    @pl.when(pl.program_id(1) == 0)
    def _():
        state_ref[...] = jnp.zeros_like(state_ref)

    ri = lax.broadcasted_iota(jnp.int32, (rows, QUAD), 0)
    ci = lax.broadcasted_iota(jnp.int32, (rows, QUAD), 1)
    head_mask = (ri // c_sz) == (ci // HEAD)
    same = (ri // c_sz) == (ci // c_sz)
    strict = same & (ci < ri)
    incl = same & (ci <= ri)
    leaf = (ri // 16) == (ci // 16)
    in32 = (ri // 32) == (ci // 32)
    eye = ri == ci
    eye_f = jnp.where(eye, 1.0, 0.0).astype(F32)

    def stack(x):
        return jnp.where(head_mask, jnp.concatenate([x] * HEADS_PER_QUAD, axis=0), 0.0)

    def tile(x):
        return jnp.concatenate([x] * HEADS_PER_QUAD, axis=0)

    def chunk_body(c, carry):
        r0 = pl.multiple_of(c * c_sz, c_sz)
        for q in range(n_quads):
            lanes = slice(q * QUAD, (q + 1) * QUAD)
            ah = ah_ref[pl.ds(r0, c_sz), lanes]
            rh = rh_ref[pl.ds(r0, c_sz), lanes]
            bh = bh_ref[pl.ds(r0, c_sz), lanes]
            kh = kh_ref[pl.ds(r0, c_sz), lanes]
            vv = v_ref[pl.ds(r0, c_sz), lanes]
            pc = pcb_ref[c, 0:1, lanes]

            ah_s = stack(ah)
            rh_s = stack(rh)
            v_s = stack(vv)
            s = _mm_nt(jnp.concatenate([ah_s, rh_s], axis=0),
                       jnp.concatenate([tile(bh), tile(kh)], axis=0))
            a_ab = jnp.where(strict, s[:rows, :rows], 0.0)
            a_ak = jnp.where(strict, s[:rows, rows:], 0.0)
            a_rb = jnp.where(incl, s[rows:, :rows], 0.0)
            a_rk = jnp.where(incl, s[rows:, rows:], 0.0)

            a_d = jnp.where(leaf, a_ab, 0.0)
            t_inv = eye_f + a_d
            pw = _mm(a_d, a_d)
            t_inv = t_inv + _mm(pw, t_inv)
            pw = _mm(pw, pw)
            t_inv = t_inv + _mm(pw, t_inv)
            pw = _mm(pw, pw)
            t_inv = t_inv + _mm(pw, t_inv)
            a_32 = jnp.where(in32 & jnp.logical_not(leaf), a_ab, 0.0)
            t_inv = t_inv + _mm(_mm(t_inv, a_32), t_inv)
            a_64 = jnp.where(in32, 0.0, a_ab)
            t_inv = t_inv + _mm(_mm(t_inv, a_64), t_inv)

            akv = _mm(a_ak, v_s)
            wu = _mm(t_inv, jnp.concatenate([ah_s, akv], axis=1))
            bt_s = stack(bh * pc)
            kt_s = stack(kh * pc)
            gh = _mm_tn(bt_s, wu)
            g_mat = gh[:, :QUAD] + jnp.where(eye, pc, 0.0)
            h_mat = gh[:, QUAD:] + _mm_tn(kt_s, v_s)
            qy = _mm(a_rb, wu)
            q_s = rh_s + qy[:, :QUAD]
            y0 = qy[:, QUAD:] + _mm(a_rk, v_s)

            m_st = state_ref[q]
            y_s = _mm(q_s, m_st) + y0
            y = y_s[0:c_sz]
            for h in range(1, HEADS_PER_QUAD):
                y = y + y_s[h * c_sz:(h + 1) * c_sz]
            y_ref[pl.ds(r0, c_sz), lanes] = y
            state_ref[q] = _mm(g_mat, m_st) + h_mat
        return carry

    lax.fori_loop(0, tb // c_sz, chunk_body, 0)

    avg_bd = jnp.where((ri // HEAD) == (ci // HEAD), 1.0 / HEAD, 0.0).astype(F32)
    for q in range(n_quads):
        lanes = slice(q * QUAD, (q + 1) * QUAD)
        y = y_ref[:, lanes]
        yc = y - _mm(y, avg_bd)
        var = _mm(yc * yc, avg_bd)
        yn = yc * lax.rsqrt(var + LNX_EPS) * lng_ref[:, lanes] + lnb_ref[:, lanes]
        o_ref[:, lanes] = ((yn + bonus_ref[:, lanes]) * g_ref[:, lanes]).astype(o_ref.dtype)


def _wkv(ah, rh, bh, kh, v, pcb, bonus, g, lnx_g, lnx_b, *, tb):
    bsz, t, rw = ah.shape
    tok = pl.BlockSpec((None, tb, rw), lambda b, i: (b, i, 0))
    row = pl.BlockSpec((1, rw), lambda b, i: (0, 0))
    return pl.pallas_call(
        _wkv_kernel,
        grid=(bsz, t // tb),
        in_specs=[tok, tok, tok, tok, tok,
                  pl.BlockSpec((None, tb // WKV_CHUNK, 8, rw), lambda b, i: (b, i, 0, 0)),
                  tok, tok, row, row],
        out_specs=tok,
        out_shape=jax.ShapeDtypeStruct((bsz, t, rw), BF16),
        scratch_shapes=[pltpu.VMEM((rw // QUAD, QUAD, QUAD), F32), pltpu.VMEM((tb, rw), F32)],
        compiler_params=pltpu.CompilerParams(
            dimension_semantics=("parallel", "arbitrary"), vmem_limit_bytes=VMEM_LIMIT),
        name="wkv7",
    )(ah, rh, bh, kh, v, pcb, bonus, g, lnx_g, lnx_b)


def _sgu_kernel(z_ref, lng_ref, lnb_ref, ws_ref, bs_ref, o_ref):
    tm, width = o_ref.shape
    u = z_ref[:, :width]
    v = z_ref[:, width:]
    mean = jnp.mean(v, axis=-1, keepdims=True)
    vc = v - mean
    var = jnp.mean(vc * vc, axis=-1, keepdims=True)
    vn = vc * lax.rsqrt(var + LN_EPS) * lng_ref[...] + lnb_ref[...]

    tr = lax.broadcasted_iota(jnp.int32, (SGU_CHUNK, SGU_CHUNK), 0)
    tc = lax.broadcasted_iota(jnp.int32, (SGU_CHUNK, SGU_CHUNK), 1)
    causal = tc <= tr
    bias = bs_ref[...]
    for g in range(width // SGU_GROUP):
        w_c = jnp.where(causal, ws_ref[g], 0.0)
        lanes = slice(g * SGU_GROUP, (g + 1) * SGU_GROUP)
        for n in range(tm // SGU_CHUNK):
            rows = slice(n * SGU_CHUNK, (n + 1) * SGU_CHUNK)
            mixed = _mm(w_c, vn[rows, lanes]) + bias[:, lanes]
            o_ref[rows, lanes] = (u[rows, lanes] * mixed).astype(o_ref.dtype)


def _sgu(z, ln_g, ln_b, w_s, b_full, *, tm):
    m, two_w = z.shape
    width = two_w // 2
    return pl.pallas_call(
        _sgu_kernel,
        grid=(m // tm,),
        in_specs=[
            pl.BlockSpec((tm, two_w), lambda i: (i, 0)),
            pl.BlockSpec((1, width), lambda i: (0, 0)),
            pl.BlockSpec((1, width), lambda i: (0, 0)),
            pl.BlockSpec(w_s.shape, lambda i: (0, 0, 0)),
            pl.BlockSpec(b_full.shape, lambda i: (0, 0)),
        ],
        out_specs=pl.BlockSpec((tm, width), lambda i: (i, 0)),
        out_shape=jax.ShapeDtypeStruct((m, width), BF16),
        compiler_params=pltpu.CompilerParams(
            dimension_semantics=("parallel",), vmem_limit_bytes=VMEM_LIMIT),
        name="sgu",
    )(z, ln_g, ln_b, w_s, b_full)


def _merge_kernel(ya_ref, yb_ref, gates_ref, x_ref, wa_ref, wb_ref, wo_ref, g2_ref, h_ref, n_ref):
    d = x_ref.shape[1]
    merged = (gates_ref[:, :d] * _mm(ya_ref[...], wa_ref[...])
              + gates_ref[:, d:] * _mm(yb_ref[...], wb_ref[...]))
    h = x_ref[...] + _mm(merged.astype(BF16), wo_ref[...])
    h_ref[...] = h
    n_ref[...] = _rmsnorm(h, g2_ref[...]).astype(BF16)


def _merge(ya, yb, gates, x, wa, wb, wo, g2, *, tm):
    m, d = x.shape
    rw = ya.shape[1]
    const = lambda a: pl.BlockSpec(a.shape, lambda i: (0, 0), pipeline_mode=pl.Buffered(1))
    return pl.pallas_call(
        _merge_kernel,
        grid=(m // tm,),
        in_specs=[
            pl.BlockSpec((tm, rw), lambda i: (i, 0)),
            pl.BlockSpec((tm, rw), lambda i: (i, 0)),
            pl.BlockSpec((tm, 2 * d), lambda i: (i, 0)),
            pl.BlockSpec((tm, d), lambda i: (i, 0)),
            const(wa), const(wb), const(wo),
            pl.BlockSpec((1, d), lambda i: (0, 0)),
        ],
        out_specs=[pl.BlockSpec((tm, d), lambda i: (i, 0)), pl.BlockSpec((tm, d), lambda i: (i, 0))],
        out_shape=[jax.ShapeDtypeStruct((m, d), F32), jax.ShapeDtypeStruct((m, d), BF16)],
        compiler_params=pltpu.CompilerParams(
            dimension_semantics=("parallel",), vmem_limit_bytes=VMEM_LIMIT),
        name="merge",
    )(ya, yb, gates, x, wa, wb, wo, g2)


def _ffn_kernel(n_ref, h_ref, wg_ref, wu_ref, wd_ref, gf_ref, o_ref, acc_ref):
    j = pl.program_id(1)

    @pl.when(j == 0)
    def _():
        acc_ref[...] = h_ref[...]

    n = n_ref[...]
    gate = _mm(n, wg_ref[...])
    up = _mm(n, wu_ref[...])
    act = (gate * jax.nn.sigmoid(gate) * up).astype(BF16)
    acc_ref[...] += _mm(act, wd_ref[...])

    @pl.when(j == pl.num_programs(1) - 1)
    def _():
        o_ref[...] = _rmsnorm(acc_ref[...], gf_ref[...])


def _ffn(n2, h1, wg, wu, wd, gf, *, tm, th):
    m, d = h1.shape
    hid = wg.shape[1]
    return pl.pallas_call(
        _ffn_kernel,
        grid=(m // tm, hid // th),
        in_specs=[
            pl.BlockSpec((tm, d), lambda i, j: (i, 0)),
            pl.BlockSpec((tm, d), lambda i, j: (i, 0)),
            pl.BlockSpec((d, th), lambda i, j: (0, j)),
            pl.BlockSpec((d, th), lambda i, j: (0, j)),
            pl.BlockSpec((th, d), lambda i, j: (j, 0)),
            pl.BlockSpec((1, d), lambda i, j: (0, 0)),
        ],
        out_specs=pl.BlockSpec((tm, d), lambda i, j: (i, 0)),
        out_shape=jax.ShapeDtypeStruct((m, d), F32),
        scratch_shapes=[pltpu.VMEM((tm, d), F32)],
        compiler_params=pltpu.CompilerParams(
            dimension_semantics=("parallel", "arbitrary"), vmem_limit_bytes=VMEM_LIMIT),
        name="ffn",
    )(n2, h1, wg, wu, wd, gf)


def _pad_cols(a, width):
    return jnp.pad(a, ((0, 0), (0, width - a.shape[1])))


def _block(h, norm_mix_g, w_in, shift_mu, w0, w_lora_up, a0, a_lora_up, g_lora_up, k_k, k_a, r_k,
           lnx_g, lnx_b, w_proj_rwkv, sgu_ln_g, sgu_ln_b, sgu_w, sgu_b, w_proj_sgu, w_out,
           norm_ffn_g, w_ffn_gate, w_ffn_up, w_ffn_down, g_last, *, bsz, seq):
    m, d = h.shape
    rw = w0.shape[0]
    dl, al, gl = w_lora_up.shape[0], a_lora_up.shape[0], g_lora_up.shape[0]
    sw = sgu_ln_g.shape[0]
    o1, o2, o3 = 3 * rw, 3 * rw + dl, 3 * rw + dl + al
    rcols = o3 + gl
    zc = rcols + 2 * sw

    def rwkv_cols(a):
        return jnp.concatenate([a[:, :o1], _pad_cols(a[:, o1:o2], LORA_PAD),
                                _pad_cols(a[:, o2:o3], LORA_PAD), a[:, o3:rcols]], axis=1)

    w_rwkv = rwkv_cols(w_in).astype(BF16)
    w_z = w_in[:, rcols:zc].astype(BF16)
    w_gates = w_in[:, zc:].astype(BF16)
    mu = rwkv_cols(shift_mu[None, :])
    wup = jnp.pad(w_lora_up, ((0, LORA_PAD - dl), (0, 0)))
    aup = jnp.pad(a_lora_up, ((0, LORA_PAD - al), (0, 0)))
    row = lambda a: a.reshape(1, -1)

    g1 = row(norm_mix_g)
    p = _norm_matmul(h, g1, w_rwkv, tm=512, tn=w_rwkv.shape[1] // 2, epilogue="none")
    z = _norm_matmul(h, g1, w_z, tm=512, tn=1024, epilogue="gelu")
    gates = _norm_matmul(h, g1, w_gates, tm=512, tn=1024, epilogue="sigmoid")

    prep = _rwkv_prep(p.reshape(bsz, seq, -1), mu, row(w0), wup, row(a0), aup, g_lora_up,
                      row(k_k), row(k_a), row(r_k), tb=256)
    ah, rh, bh, kh, v, bonus, g, pcb = prep
    ya = _wkv(ah, rh, bh, kh, v, pcb, bonus, g, row(lnx_g), row(lnx_b), tb=256).reshape(m, rw)

    b_full = jnp.repeat(sgu_b.T, SGU_GROUP, axis=1)
    yb = _sgu(z, row(sgu_ln_g), row(sgu_ln_b), sgu_w, b_full, tm=512)

    h1, n2 = _merge(ya, yb, gates, h, w_proj_rwkv.astype(BF16), w_proj_sgu.astype(BF16),
                    w_out.astype(BF16), row(norm_ffn_g), tm=256)
    return _ffn(n2, h1, w_ffn_gate.astype(BF16), w_ffn_up.astype(BF16), w_ffn_down.astype(BF16),
                row(g_last), tm=512, th=512)


def kernel(x, norm_mix_g, w_in, shift_mu, w0, w_lora_up, a0, a_lora_up, g_lora_up, k_k, k_a, r_k,
           lnx_g, lnx_b, w_proj_rwkv, sgu_ln_g, sgu_ln_b, sgu_w, sgu_b, w_proj_sgu, w_out,
           norm_ffn_g, w_ffn_gate, w_ffn_up, w_ffn_down, norm_final_g):
    bsz, seq, d = x.shape
    depth = w_in.shape[0]
    assert depth == 1, "the final rmsnorm is fused into the (single) layer's channel mixer"
    h = x.reshape(bsz * seq, d)
    out = _block(h, norm_mix_g[0], w_in[0], shift_mu[0], w0[0], w_lora_up[0], a0[0], a_lora_up[0],
                 g_lora_up[0], k_k[0], k_a[0], r_k[0], lnx_g[0], lnx_b[0], w_proj_rwkv[0],
                 sgu_ln_g[0], sgu_ln_b[0], sgu_w[0], sgu_b[0], w_proj_sgu[0], w_out[0],
                 norm_ffn_g[0], w_ffn_gate[0], w_ffn_up[0], w_ffn_down[0], norm_final_g,
                 bsz=bsz, seq=seq)
    return out.reshape(bsz, seq, d)
```

```python
import functools

import jax
import jax.numpy as jnp
from jax import lax
from jax.experimental import pallas as pl
from jax.experimental.pallas import tpu as pltpu

F32 = jnp.float32
BF16 = jnp.bfloat16

RMS_EPS = 1e-6
LN_EPS = 1e-5
LNX_EPS = 64e-5

HEAD = 64
WKV_CHUNK = 64
QUAD = 256
HEADS_PER_QUAD = QUAD // HEAD
SGU_CHUNK = 128
SGU_GROUP = 128
LANE = 128

VMEM_LIMIT = 56 * 1024 * 1024


def _mm(a, b):
    return jnp.dot(a, b, preferred_element_type=F32)


def _mm_nt(a, b):
    return lax.dot_general(a, b, (((1,), (1,)), ((), ())), preferred_element_type=F32)


def _mm_tn(a, b):
    return lax.dot_general(a, b, (((0,), (0,)), ((), ())), preferred_element_type=F32)


def _rmsnorm(x, g):
    return x * lax.rsqrt(jnp.mean(x * x, axis=-1, keepdims=True) + RMS_EPS) * g


def _rmsnorm_kernel(x_ref, g_ref, o_ref):
    o_ref[...] = _rmsnorm(x_ref[...], g_ref[...]).astype(o_ref.dtype)


def _rmsnorm_call(x, g, *, tm):
    m, d = x.shape
    return pl.pallas_call(
        _rmsnorm_kernel,
        grid=(m // tm,),
        in_specs=[pl.BlockSpec((tm, d), lambda i: (i, 0)), pl.BlockSpec((1, d), lambda i: (0, 0))],
        out_specs=pl.BlockSpec((tm, d), lambda i: (i, 0)),
        out_shape=jax.ShapeDtypeStruct((m, d), BF16),
        compiler_params=pltpu.CompilerParams(dimension_semantics=("parallel",)),
        name="rmsnorm",
    )(x, g)


def _proj_kernel(n_ref, w_ref, o_ref, *, epilogue, sub):
    for k in range(o_ref.shape[1] // sub):
        cols = slice(k * sub, (k + 1) * sub)
        y = _mm(n_ref[...], w_ref[:, cols])
        if epilogue == "gelu":
            y = 0.5 * y * (1.0 + lax.erf(y * (2.0 ** -0.5)))
        elif epilogue == "sigmoid":
            y = jax.nn.sigmoid(y)
        o_ref[:, cols] = y.astype(o_ref.dtype)


def _proj(n, w, n_cols, *, tm, tn, epilogue, sub=256):
    m, d = n.shape
    return pl.pallas_call(
        functools.partial(_proj_kernel, epilogue=epilogue, sub=sub),
        grid=(m // tm, n_cols // tn),
        in_specs=[
            pl.BlockSpec((tm, d), lambda i, j: (i, 0)),
            pl.BlockSpec((d, tn), lambda i, j: (0, j)),
        ],
        out_specs=pl.BlockSpec((tm, tn), lambda i, j: (i, j)),
        out_shape=jax.ShapeDtypeStruct((m, n_cols), F32),
        compiler_params=pltpu.CompilerParams(
            dimension_semantics=("parallel", "arbitrary"), vmem_limit_bytes=VMEM_LIMIT),
        name="proj_" + epilogue,
    )(n, w)


def _head_sum(x, ones_bd):
    parts = [_mm(x[:, q * QUAD:(q + 1) * QUAD], ones_bd) for q in range(x.shape[1] // QUAD)]
    return jnp.concatenate(parts, axis=1)


def _rwkv_prep_kernel(p_ref, mu_ref, w0_ref, wup_ref, a0_ref, aup_ref, gup_ref, kk_ref, ka_ref,
                      rk_ref, ah_ref, rh_ref, bh_ref, kh_ref, v_ref, bonus_ref, g_ref, pcb_ref,
                      carry_ref, *, rw, win_w, win_a, win_g):
    tb = p_ref.shape[0]

    @pl.when(pl.program_id(1) == 0)
    def _():
        carry_ref[...] = jnp.zeros_like(carry_ref)

    p = p_ref[...]
    row = lax.broadcasted_iota(jnp.int32, p.shape, 0)
    prev = jnp.where(row == 0, carry_ref[0:1, :], pltpu.roll(p, 1, 0))
    carry_ref[0:1, :] = p[tb - 1:tb, :]
    sh = p + (prev - p) * mu_ref[...]

    r = sh[:, 0:rw]
    k = sh[:, rw:2 * rw]
    v = sh[:, 2 * rw:3 * rw]
    xw = sh[:, win_w[0]:win_w[1]]
    xa = sh[:, win_a[0]:win_a[1]]
    xg = sh[:, win_g[0]:win_g[1]]

    wl = w0_ref[...] + _mm(jnp.tanh(xw), wup_ref[...])
    z = -wl
    softplus = jnp.maximum(z, 0.0) + jnp.log1p(jnp.exp(-jnp.abs(z)))
    lw = -jnp.exp(-softplus - 0.5)
    a = jax.nn.sigmoid(a0_ref[...] + _mm(xa, aup_ref[...]))
    g_ref[...] = _mm(jax.nn.sigmoid(xg), gup_ref[...])

    qr = lax.broadcasted_iota(jnp.int32, (QUAD, QUAD), 0)
    qc = lax.broadcasted_iota(jnp.int32, (QUAD, QUAD), 1)
    ones_bd = jnp.where(qr // HEAD == qc // HEAD, 1.0, 0.0).astype(F32)

    kk = k * kk_ref[...]
    nrm = jnp.maximum(jnp.sqrt(_head_sum(kk * kk, ones_bd)), 1e-12)
    kk = kk / nrm
    k2 = k * (1.0 + (a - 1.0) * ka_ref[...])
    bonus_ref[...] = _head_sum(r * k2 * rk_ref[...], ones_bd) * v
    v_ref[...] = v

    tr = lax.broadcasted_iota(jnp.int32, (tb, tb), 0)
    tc = lax.broadcasted_iota(jnp.int32, (tb, tb), 1)
    tri = jnp.where((tr // WKV_CHUNK == tc // WKV_CHUNK) & (tc <= tr), 1.0, 0.0).astype(BF16)
    h1 = lw.astype(BF16)
    r1 = lw - h1.astype(F32)
    h2 = r1.astype(BF16)
    h3 = (r1 - h2.astype(F32)).astype(BF16)
    cum = _mm(tri, h1) + _mm(tri, h2) + _mm(tri, h3)

    ec = jnp.exp(cum)
    eci = jnp.exp(-cum)
    ah_ref[...] = -kk * jnp.exp(cum - lw)
    rh_ref[...] = r * ec
    bh_ref[...] = kk * a * eci
    kh_ref[...] = k2 * eci
    for c in range(tb // WKV_CHUNK):
        last = (c + 1) * WKV_CHUNK - 1
        pcb_ref[c] = jnp.broadcast_to(ec[last:last + 1, :], (8, rw))


def _lane_window(start, stop):
    return (start // LANE) * LANE, -(-stop // LANE) * LANE


def _pad_rows_to_window(w, start, window):
    return jnp.pad(w, ((start - window[0], window[1] - start - w.shape[0]), (0, 0)))


def _rwkv_prep(p, mu, w0, w_lora_up, a0, a_lora_up, g_lora_up, k_k, k_a, r_k, *, tb):
    bsz, t, cols = p.shape
    rw = w0.shape[1]
    nchunk = t // WKV_CHUNK
    o1 = 3 * rw
    o2 = o1 + w_lora_up.shape[0]
    o3 = o2 + a_lora_up.shape[0]
    o4 = o3 + g_lora_up.shape[0]
    win_w, win_a, win_g = _lane_window(o1, o2), _lane_window(o2, o3), _lane_window(o3, o4)
    assert win_g[1] == cols
    wup = _pad_rows_to_window(w_lora_up, o1, win_w)
    aup = _pad_rows_to_window(a_lora_up, o2, win_a)
    gup = _pad_rows_to_window(g_lora_up, o3, win_g)
    row_spec = lambda w: pl.BlockSpec((1, w), lambda b, i: (0, 0))
    full = lambda a: pl.BlockSpec(a.shape, lambda b, i: (0, 0))
    tok = pl.BlockSpec((None, tb, rw), lambda b, i: (b, i, 0))
    out_tok = jax.ShapeDtypeStruct((bsz, t, rw), F32)
    return pl.pallas_call(
        functools.partial(_rwkv_prep_kernel, rw=rw, win_w=win_w, win_a=win_a, win_g=win_g),
        grid=(bsz, t // tb),
        in_specs=[
            pl.BlockSpec((None, tb, cols), lambda b, i: (b, i, 0)),
            row_spec(cols), row_spec(rw), full(wup), row_spec(rw), full(aup), full(gup),
            row_spec(rw), row_spec(rw), row_spec(rw),
        ],
        out_specs=[tok] * 7 + [
            pl.BlockSpec((None, tb // WKV_CHUNK, 8, rw), lambda b, i: (b, i, 0, 0))],
        out_shape=[out_tok] * 7 + [jax.ShapeDtypeStruct((bsz, nchunk, 8, rw), F32)],
        scratch_shapes=[pltpu.VMEM((8, cols), F32)],
        compiler_params=pltpu.CompilerParams(
            dimension_semantics=("parallel", "arbitrary"), vmem_limit_bytes=VMEM_LIMIT),
        name="rwkv_prep",
    )(p, mu, w0, wup, a0, aup, gup, k_k, k_a, r_k)


def _wkv_kernel(ah_ref, rh_ref, bh_ref, kh_ref, v_ref, pcb_ref, bonus_ref, g_ref, lng_ref,
                lnb_ref, o_ref, state_ref, y_ref):
    tb, rw = o_ref.shape
    c_sz = WKV_CHUNK
    n_quads = rw // QUAD
    rows = HEADS_PER_QUAD * c_sz
    assert rows == QUAD

    @pl.when(pl.program_id(1) == 0)
    def _():
        state_ref[...] = jnp.zeros_like(state_ref)

    ri = lax.broadcasted_iota(jnp.int32, (rows, QUAD), 0)
    ci = lax.broadcasted_iota(jnp.int32, (rows, QUAD), 1)
    head_mask = (ri // c_sz) == (ci // HEAD)
    same = (ri // c_sz) == (ci // c_sz)
    strict = same & (ci < ri)
    incl = same & (ci <= ri)
    leaf = (ri // 16) == (ci // 16)
    in32 = (ri // 32) == (ci // 32)
    eye = ri == ci
    eye_f = jnp.where(eye, 1.0, 0.0).astype(F32)

    def stack(x):
        return jnp.where(head_mask, jnp.concatenate([x] * HEADS_PER_QUAD, axis=0), 0.0)

    def expand(x):
        return jnp.concatenate([x, x], axis=1)

    def chunk_body(c, carry):
        r0 = pl.multiple_of(c * c_sz, c_sz)
        qs = range(n_quads)
        lanes = [slice(q * QUAD, (q + 1) * QUAD) for q in qs]
        ah = [ah_ref[pl.ds(r0, c_sz), lanes[q]] for q in qs]
        rh = [rh_ref[pl.ds(r0, c_sz), lanes[q]] for q in qs]
        bh = [bh_ref[pl.ds(r0, c_sz), lanes[q]] for q in qs]
        kh = [kh_ref[pl.ds(r0, c_sz), lanes[q]] for q in qs]
        vv = [v_ref[pl.ds(r0, c_sz), lanes[q]] for q in qs]
        pc = [pcb_ref[c, 0:1, lanes[q]] for q in qs]

        ah_s = [stack(x) for x in ah]
        rh_s = [stack(x) for x in rh]
        v_s = [stack(x) for x in vv]
        s = [_mm_nt(jnp.concatenate([ah_s[q], rh_s[q]], axis=0),
                    jnp.concatenate([bh[q], bh[q], kh[q], kh[q]], axis=0)) for q in qs]
        a_ab = [jnp.where(strict, expand(s[q][:rows, :QUAD // 2]), 0.0) for q in qs]
        a_ak = [jnp.where(strict, expand(s[q][:rows, QUAD // 2:]), 0.0) for q in qs]
        a_rb = [jnp.where(incl, expand(s[q][rows:, :QUAD // 2]), 0.0) for q in qs]
        a_rk = [jnp.where(incl, expand(s[q][rows:, QUAD // 2:]), 0.0) for q in qs]

        a_d = [jnp.where(leaf, a_ab[q], 0.0) for q in qs]
        t_inv = [eye_f + a_d[q] for q in qs]
        pw = [_mm(a_d[q], a_d[q]) for q in qs]
        t_inv = [t_inv[q] + _mm(pw[q], t_inv[q]) for q in qs]
        pw = [_mm(pw[q], pw[q]) for q in qs]
        t_inv = [t_inv[q] + _mm(pw[q], t_inv[q]) for q in qs]
        pw = [_mm(pw[q], pw[q]) for q in qs]
        t_inv = [t_inv[q] + _mm(pw[q], t_inv[q]) for q in qs]
        a_32 = [jnp.where(in32 & jnp.logical_not(leaf), a_ab[q], 0.0) for q in qs]
        ta = [_mm(t_inv[q], a_32[q]) for q in qs]
        t_inv = [t_inv[q] + _mm(ta[q], t_inv[q]) for q in qs]
        a_64 = [jnp.where(in32, 0.0, a_ab[q]) for q in qs]
        ta = [_mm(t_inv[q], a_64[q]) for q in qs]
        t_inv = [t_inv[q] + _mm(ta[q], t_inv[q]) for q in qs]

        akv = [_mm(a_ak[q], v_s[q]) for q in qs]
        wu = [_mm(t_inv[q], jnp.concatenate([ah_s[q], akv[q]], axis=1)) for q in qs]
        bkd = [jnp.concatenate([stack(bh[q] * pc[q]), stack(kh[q] * pc[q]),
                                jnp.where(eye, pc[q], 0.0)], axis=0) for q in qs]

        m_st = [state_ref[q] for q in qs]
        u_s = [_mm(wu[q][:, :QUAD], m_st[q]) + wu[q][:, QUAD:] for q in qs]
        muv = [jnp.concatenate([m_st[q], u_s[q], v_s[q]], axis=0) for q in qs]
        y_s = [_mm(jnp.concatenate([rh_s[q], a_rb[q], a_rk[q]], axis=1), muv[q]) for q in qs]
        for q in qs:
            y = y_s[q][0:c_sz]
            for h in range(1, HEADS_PER_QUAD):
                y = y + y_s[q][h * c_sz:(h + 1) * c_sz]
            y_ref[pl.ds(r0, c_sz), lanes[q]] = y
            state_ref[q] = _mm_tn(bkd[q], jnp.concatenate([u_s[q], v_s[q], m_st[q]], axis=0))
        return carry

    lax.fori_loop(0, tb // c_sz, chunk_body, 0)

    avg_bd = jnp.where((ri // HEAD) == (ci // HEAD), 1.0 / HEAD, 0.0).astype(F32)
    for q in range(n_quads):
        lanes = slice(q * QUAD, (q + 1) * QUAD)
        y = y_ref[:, lanes]
        yc = y - _mm(y, avg_bd)
        var = _mm(yc * yc, avg_bd)
        yn = yc * lax.rsqrt(var + LNX_EPS) * lng_ref[:, lanes] + lnb_ref[:, lanes]
        o_ref[:, lanes] = ((yn + bonus_ref[:, lanes]) * g_ref[:, lanes]).astype(o_ref.dtype)


def _wkv(ah, rh, bh, kh, v, pcb, bonus, g, lnx_g, lnx_b, *, tb):
    bsz, t, rw = ah.shape
    tok = pl.BlockSpec((None, tb, rw), lambda b, i: (b, i, 0))
    row = pl.BlockSpec((1, rw), lambda b, i: (0, 0))
    return pl.pallas_call(
        _wkv_kernel,
        grid=(bsz, t // tb),
        in_specs=[tok, tok, tok, tok, tok,
                  pl.BlockSpec((None, tb // WKV_CHUNK, 8, rw), lambda b, i: (b, i, 0, 0)),
                  tok, tok, row, row],
        out_specs=tok,
        out_shape=jax.ShapeDtypeStruct((bsz, t, rw), BF16),
        scratch_shapes=[pltpu.VMEM((rw // QUAD, QUAD, QUAD), F32), pltpu.VMEM((tb, rw), F32)],
        compiler_params=pltpu.CompilerParams(
            dimension_semantics=("parallel", "arbitrary"), vmem_limit_bytes=VMEM_LIMIT),
        name="wkv7",
    )(ah, rh, bh, kh, v, pcb, bonus, g, lnx_g, lnx_b)


def _sgu_kernel(z_ref, lng_ref, lnb_ref, ws_ref, bs_ref, o_ref):
    tm, width = o_ref.shape
    u = z_ref[:, :width]
    v = z_ref[:, width:]
    mean = jnp.mean(v, axis=-1, keepdims=True)
    vc = v - mean
    var = jnp.mean(vc * vc, axis=-1, keepdims=True)
    vn = vc * lax.rsqrt(var + LN_EPS) * lng_ref[...] + lnb_ref[...]

    tr = lax.broadcasted_iota(jnp.int32, (SGU_CHUNK, SGU_CHUNK), 0)
    tc = lax.broadcasted_iota(jnp.int32, (SGU_CHUNK, SGU_CHUNK), 1)
    causal = tc <= tr
    bias = bs_ref[...]
    for g in range(width // SGU_GROUP):
        w_c = jnp.where(causal, ws_ref[g], 0.0)
        lanes = slice(g * SGU_GROUP, (g + 1) * SGU_GROUP)
        for n in range(tm // SGU_CHUNK):
            rows = slice(n * SGU_CHUNK, (n + 1) * SGU_CHUNK)
            mixed = _mm(w_c, vn[rows, lanes]) + bias[:, lanes]
            o_ref[rows, lanes] = (u[rows, lanes] * mixed).astype(o_ref.dtype)


def _sgu(z, ln_g, ln_b, w_s, b_full, *, tm):
    m, two_w = z.shape
    width = two_w // 2
    return pl.pallas_call(
        _sgu_kernel,
        grid=(m // tm,),
        in_specs=[
            pl.BlockSpec((tm, two_w), lambda i: (i, 0)),
            pl.BlockSpec((1, width), lambda i: (0, 0)),
            pl.BlockSpec((1, width), lambda i: (0, 0)),
            pl.BlockSpec(w_s.shape, lambda i: (0, 0, 0)),
            pl.BlockSpec(b_full.shape, lambda i: (0, 0)),
        ],
        out_specs=pl.BlockSpec((tm, width), lambda i: (i, 0)),
        out_shape=jax.ShapeDtypeStruct((m, width), BF16),
        compiler_params=pltpu.CompilerParams(
            dimension_semantics=("parallel",), vmem_limit_bytes=VMEM_LIMIT),
        name="sgu",
    )(z, ln_g, ln_b, w_s, b_full)


def _merge_kernel(ya_ref, yb_ref, gates_ref, x_ref, wa_ref, wb_ref, wo_ref, g2_ref, h_ref, n_ref):
    d = x_ref.shape[1]
    merged = (gates_ref[:, :d] * _mm(ya_ref[...], wa_ref[...])
              + gates_ref[:, d:] * _mm(yb_ref[...], wb_ref[...]))
    h = x_ref[...] + _mm(merged.astype(BF16), wo_ref[...])
    h_ref[...] = h
    n_ref[...] = _rmsnorm(h, g2_ref[...]).astype(BF16)


def _merge(ya, yb, gates, x, wa, wb, wo, g2, *, tm):
    m, d = x.shape
    rw = ya.shape[1]
    const = lambda a: pl.BlockSpec(a.shape, lambda i: (0, 0), pipeline_mode=pl.Buffered(1))
    return pl.pallas_call(
        _merge_kernel,
        grid=(m // tm,),
        in_specs=[
            pl.BlockSpec((tm, rw), lambda i: (i, 0)),
            pl.BlockSpec((tm, rw), lambda i: (i, 0)),
            pl.BlockSpec((tm, 2 * d), lambda i: (i, 0)),
            pl.BlockSpec((tm, d), lambda i: (i, 0)),
            const(wa), const(wb), const(wo),
            pl.BlockSpec((1, d), lambda i: (0, 0)),
        ],
        out_specs=[pl.BlockSpec((tm, d), lambda i: (i, 0)), pl.BlockSpec((tm, d), lambda i: (i, 0))],
        out_shape=[jax.ShapeDtypeStruct((m, d), F32), jax.ShapeDtypeStruct((m, d), BF16)],
        compiler_params=pltpu.CompilerParams(
            dimension_semantics=("parallel",), vmem_limit_bytes=VMEM_LIMIT),
        name="merge",
    )(ya, yb, gates, x, wa, wb, wo, g2)


def _ffn_kernel(n_ref, h_ref, wg_ref, wu_ref, wd_ref, gf_ref, o_ref, acc_ref):
    j = pl.program_id(1)

    @pl.when(j == 0)
    def _():
        acc_ref[...] = h_ref[...]

    n = n_ref[...]
    gate = _mm(n, wg_ref[...])
    up = _mm(n, wu_ref[...])
    act = (gate * jax.nn.sigmoid(gate) * up).astype(BF16)
    acc_ref[...] += _mm(act, wd_ref[...])

    @pl.when(j == pl.num_programs(1) - 1)
    def _():
        o_ref[...] = _rmsnorm(acc_ref[...], gf_ref[...])


def _ffn(n2, h1, wg, wu, wd, gf, *, tm, th):
    m, d = h1.shape
    hid = wg.shape[1]
    return pl.pallas_call(
        _ffn_kernel,
        grid=(m // tm, hid // th),
        in_specs=[
            pl.BlockSpec((tm, d), lambda i, j: (i, 0)),
            pl.BlockSpec((tm, d), lambda i, j: (i, 0)),
            pl.BlockSpec((d, th), lambda i, j: (0, j)),
            pl.BlockSpec((d, th), lambda i, j: (0, j)),
            pl.BlockSpec((th, d), lambda i, j: (j, 0)),
            pl.BlockSpec((1, d), lambda i, j: (0, 0)),
        ],
        out_specs=pl.BlockSpec((tm, d), lambda i, j: (i, 0)),
        out_shape=jax.ShapeDtypeStruct((m, d), F32),
        scratch_shapes=[pltpu.VMEM((tm, d), F32)],
        compiler_params=pltpu.CompilerParams(
            dimension_semantics=("parallel", "arbitrary"), vmem_limit_bytes=VMEM_LIMIT),
        name="ffn",
    )(n2, h1, wg, wu, wd, gf)


def _pad_cols(a, width):
    return jnp.pad(a, ((0, 0), (0, width - a.shape[1])))


def _block(h, norm_mix_g, w_in, shift_mu, w0, w_lora_up, a0, a_lora_up, g_lora_up, k_k, k_a, r_k,
           lnx_g, lnx_b, w_proj_rwkv, sgu_ln_g, sgu_ln_b, sgu_w, sgu_b, w_proj_sgu, w_out,
           norm_ffn_g, w_ffn_gate, w_ffn_up, w_ffn_down, g_last, *, bsz, seq):
    m, d = h.shape
    rw = w0.shape[0]
    sw = sgu_ln_g.shape[0]
    rcols = shift_mu.shape[0]
    zc = rcols + 2 * sw
    rcols_al = -(-rcols // LANE) * LANE
    row = lambda a: a.reshape(1, -1)

    w_in16 = w_in.astype(BF16)
    w_z = w_in16[:, rcols:zc]
    w_gates = w_in16[:, zc:]
    mu = _pad_cols(shift_mu[None, :], rcols_al)

    n1 = _rmsnorm_call(h, row(norm_mix_g), tm=512)
    p = _proj(n1, w_in16, rcols_al, tm=1024, tn=rcols_al // 2, epilogue="none")
    z = _proj(n1, w_z, 2 * sw, tm=1024, tn=1024, epilogue="gelu")
    gates = _proj(n1, w_gates, 2 * d, tm=1024, tn=1024, epilogue="sigmoid")

    prep = _rwkv_prep(p.reshape(bsz, seq, -1), mu, row(w0), w_lora_up, row(a0), a_lora_up,
                      g_lora_up, row(k_k), row(k_a), row(r_k), tb=256)
    ah, rh, bh, kh, v, bonus, g, pcb = prep
    ya = _wkv(ah, rh, bh, kh, v, pcb, bonus, g, row(lnx_g), row(lnx_b), tb=256).reshape(m, rw)

    b_full = jnp.repeat(sgu_b.T, SGU_GROUP, axis=1)
    yb = _sgu(z, row(sgu_ln_g), row(sgu_ln_b), sgu_w, b_full, tm=512)

    h1, n2 = _merge(ya, yb, gates, h, w_proj_rwkv.astype(BF16), w_proj_sgu.astype(BF16),
                    w_out.astype(BF16), row(norm_ffn_g), tm=256)
    return _ffn(n2, h1, w_ffn_gate.astype(BF16), w_ffn_up.astype(BF16), w_ffn_down.astype(BF16),
                row(g_last), tm=512, th=512)


def kernel(x, norm_mix_g, w_in, shift_mu, w0, w_lora_up, a0, a_lora_up, g_lora_up, k_k, k_a, r_k,
           lnx_g, lnx_b, w_proj_rwkv, sgu_ln_g, sgu_ln_b, sgu_w, sgu_b, w_proj_sgu, w_out,
           norm_ffn_g, w_ffn_gate, w_ffn_up, w_ffn_down, norm_final_g):
    bsz, seq, d = x.shape
    depth = w_in.shape[0]
    assert depth == 1, "the final rmsnorm is fused into the (single) layer's channel mixer"
    h = x.reshape(bsz * seq, d)
    out = _block(h, norm_mix_g[0], w_in[0], shift_mu[0], w0[0], w_lora_up[0], a0[0], a_lora_up[0],
                 g_lora_up[0], k_k[0], k_a[0], r_k[0], lnx_g[0], lnx_b[0], w_proj_rwkv[0],
                 sgu_ln_g[0], sgu_ln_b[0], sgu_w[0], sgu_b[0], w_proj_sgu[0], w_out[0],
                 norm_ffn_g[0], w_ffn_gate[0], w_ffn_up[0], w_ffn_down[0], norm_final_g,
                 bsz=bsz, seq=seq)
    return out.reshape(bsz, seq, d)
```

```python
import functools

import jax
import jax.numpy as jnp
from jax import lax
from jax.experimental import pallas as pl
from jax.experimental.pallas import tpu as pltpu

F32 = jnp.float32
BF16 = jnp.bfloat16

RMS_EPS = 1e-6
LN_EPS = 1e-5
LNX_EPS = 64e-5

HEAD = 64
WKV_CHUNK = 64
QUAD = 256
HEADS_PER_QUAD = QUAD // HEAD
SGU_CHUNK = 128
SGU_GROUP = 128
LANE = 128

VMEM_LIMIT = 56 * 1024 * 1024


def _mm(a, b):
    return jnp.dot(a, b, preferred_element_type=F32)


def _mm_nt(a, b):
    return lax.dot_general(a, b, (((1,), (1,)), ((), ())), preferred_element_type=F32)


def _mm_tn(a, b):
    return lax.dot_general(a, b, (((0,), (0,)), ((), ())), preferred_element_type=F32)


def _rmsnorm(x, g):
    return x * lax.rsqrt(jnp.mean(x * x, axis=-1, keepdims=True) + RMS_EPS) * g


def _rmsnorm_kernel(x_ref, g_ref, o_ref):
    o_ref[...] = _rmsnorm(x_ref[...], g_ref[...]).astype(o_ref.dtype)


def _rmsnorm_call(x, g, *, tm):
    m, d = x.shape
    return pl.pallas_call(
        _rmsnorm_kernel,
        grid=(m // tm,),
        in_specs=[pl.BlockSpec((tm, d), lambda i: (i, 0)), pl.BlockSpec((1, d), lambda i: (0, 0))],
        out_specs=pl.BlockSpec((tm, d), lambda i: (i, 0)),
        out_shape=jax.ShapeDtypeStruct((m, d), BF16),
        compiler_params=pltpu.CompilerParams(dimension_semantics=("parallel",)),
        name="rmsnorm",
    )(x, g)


def _proj_kernel(n_ref, w_ref, o_ref, *, epilogue, sub):
    for k in range(o_ref.shape[1] // sub):
        cols = slice(k * sub, (k + 1) * sub)
        y = _mm(n_ref[...], w_ref[:, cols])
        if epilogue == "gelu":
            y = 0.5 * y * (1.0 + lax.erf(y * (2.0 ** -0.5)))
        elif epilogue == "sigmoid":
            y = jax.nn.sigmoid(y)
        o_ref[:, cols] = y.astype(o_ref.dtype)


def _proj(n, w, n_cols, *, tm, tn, epilogue, sub=256):
    m, d = n.shape
    return pl.pallas_call(
        functools.partial(_proj_kernel, epilogue=epilogue, sub=sub),
        grid=(m // tm, n_cols // tn),
        in_specs=[
            pl.BlockSpec((tm, d), lambda i, j: (i, 0)),
            pl.BlockSpec((d, tn), lambda i, j: (0, j)),
        ],
        out_specs=pl.BlockSpec((tm, tn), lambda i, j: (i, j)),
        out_shape=jax.ShapeDtypeStruct((m, n_cols), F32),
        compiler_params=pltpu.CompilerParams(
            dimension_semantics=("parallel", "arbitrary"), vmem_limit_bytes=VMEM_LIMIT),
        name="proj_" + epilogue,
    )(n, w)


def _rwkv_mix_kernel(p_ref, mu_ref, w0_ref, wup_ref, a0_ref, aup_ref, gup_ref, kk_ref, ka_ref,
                     rk_ref, lng_ref, lnb_ref, o_ref, carry_ref, state_ref, *, rw, win_w, win_a,
                     win_g):
    tb = p_ref.shape[0]
    c_sz = WKV_CHUNK
    n_quads = rw // QUAD
    qs = range(n_quads)
    lanes = [slice(q * QUAD, (q + 1) * QUAD) for q in qs]
    rows = HEADS_PER_QUAD * c_sz
    assert rows == QUAD

    @pl.when(pl.program_id(1) == 0)
    def _():
        carry_ref[...] = jnp.zeros_like(carry_ref)
        state_ref[...] = jnp.zeros_like(state_ref)

    ri = lax.broadcasted_iota(jnp.int32, (rows, QUAD), 0)
    ci = lax.broadcasted_iota(jnp.int32, (rows, QUAD), 1)
    ones_bd = jnp.where((ri // HEAD) == (ci // HEAD), 1.0, 0.0).astype(F32)
    avg_bd = ones_bd * (1.0 / HEAD)
    head_mask = (ri // c_sz) == (ci // HEAD)
    same = (ri // c_sz) == (ci // c_sz)
    strict = same & (ci < ri)
    incl = same & (ci <= ri)
    leaf = (ri // 16) == (ci // 16)
    in32 = (ri // 32) == (ci // 32)
    eye = ri == ci
    eye_f = jnp.where(eye, 1.0, 0.0).astype(F32)
    tr = lax.broadcasted_iota(jnp.int32, (c_sz, c_sz), 0)
    tc = lax.broadcasted_iota(jnp.int32, (c_sz, c_sz), 1)
    tri = jnp.where(tc <= tr, 1.0, 0.0).astype(BF16)
    crow = lax.broadcasted_iota(jnp.int32, (c_sz, 1), 0)

    def head_sums(xs):
        lhs = jnp.concatenate([x[:, lanes[q]] for x in xs for q in qs], axis=0)
        out = _mm(lhs, ones_bd)
        return [jnp.concatenate([out[(i * n_quads + q) * c_sz:(i * n_quads + q + 1) * c_sz]
                                 for q in qs], axis=1) for i in range(len(xs))]

    def stack(x):
        return jnp.where(head_mask, jnp.concatenate([x] * HEADS_PER_QUAD, axis=0), 0.0)

    def expand(x):
        return jnp.concatenate([x, x], axis=1)

    def prep_chunk(c):
        r0 = c * c_sz
        p = p_ref[r0:r0 + c_sz, :]
        before = carry_ref[0:1, :] if c == 0 else p_ref[r0 - 1:r0, :]
        prev = jnp.where(crow == 0, before, pltpu.roll(p, 1, 0))
        sh = p + (prev - p) * mu_ref[...]
        r = sh[:, 0:rw]
        k = sh[:, rw:2 * rw]
        v = sh[:, 2 * rw:3 * rw]
        xw = sh[:, win_w[0]:win_w[1]]
        xa = sh[:, win_a[0]:win_a[1]]
        xg = sh[:, win_g[0]:win_g[1]]

        wl = w0_ref[...] + _mm(jnp.tanh(xw), wup_ref[...])
        lw = (-(2.718281828459045 ** -0.5)) * jax.nn.sigmoid(wl)
        a = jax.nn.sigmoid(a0_ref[...] + _mm(xa, aup_ref[...]))
        g = _mm(jax.nn.sigmoid(xg), gup_ref[...])

        kk = k * kk_ref[...]
        k2 = k * (1.0 + (a - 1.0) * ka_ref[...])
        kk_ss, rk_sum = head_sums([kk * kk, r * k2 * rk_ref[...]])
        kk = kk * jnp.minimum(lax.rsqrt(kk_ss), 1e12)
        bonus = rk_sum * v

        h1 = lw.astype(BF16)
        r1 = lw - h1.astype(F32)
        h2 = r1.astype(BF16)
        h3 = (r1 - h2.astype(F32)).astype(BF16)
        cum = _mm(tri, h1) + _mm(tri, h2) + _mm(tri, h3)
        ec = jnp.exp(cum)
        eci = jnp.exp(-cum)
        ec_prev = jnp.where(crow == 0, 1.0, pltpu.roll(ec, 1, 0))
        return dict(ah=-kk * ec_prev, rh=r * ec, bh=kk * a * eci, kh=k2 * eci, v=v,
                    pc=ec[c_sz - 1:c_sz, :], bonus=bonus, g=g)

    def wkv_chunk(d):
        ah = [d["ah"][:, lanes[q]] for q in qs]
        rh = [d["rh"][:, lanes[q]] for q in qs]
        bh = [d["bh"][:, lanes[q]] for q in qs]
        kh = [d["kh"][:, lanes[q]] for q in qs]
        vv = [d["v"][:, lanes[q]] for q in qs]
        pc = [d["pc"][:, lanes[q]] for q in qs]

        ah_s = [stack(x) for x in ah]
        rh_s = [stack(x) for x in rh]
        v_s = [stack(x) for x in vv]
        s = [_mm_nt(jnp.concatenate([ah_s[q], rh_s[q]], axis=0),
                    jnp.concatenate([bh[q], bh[q], kh[q], kh[q]], axis=0)) for q in qs]
        a_ab = [jnp.where(strict, expand(s[q][:rows, :QUAD // 2]), 0.0) for q in qs]
        a_ak = [jnp.where(strict, expand(s[q][:rows, QUAD // 2:]), 0.0) for q in qs]
        a_rb = [jnp.where(incl, expand(s[q][rows:, :QUAD // 2]), 0.0) for q in qs]
        a_rk = [jnp.where(incl, expand(s[q][rows:, QUAD // 2:]), 0.0) for q in qs]

        a_d = [jnp.where(leaf, a_ab[q], 0.0) for q in qs]
        t_inv = [eye_f + a_d[q] for q in qs]
        pw = [_mm(a_d[q], a_d[q]) for q in qs]
        t_inv = [t_inv[q] + _mm(pw[q], t_inv[q]) for q in qs]
        pw = [_mm(pw[q], pw[q]) for q in qs]
        t_inv = [t_inv[q] + _mm(pw[q], t_inv[q]) for q in qs]
        pw = [_mm(pw[q], pw[q]) for q in qs]
        t_inv = [t_inv[q] + _mm(pw[q], t_inv[q]) for q in qs]
        a_32 = [jnp.where(in32 & jnp.logical_not(leaf), a_ab[q], 0.0) for q in qs]
        ta = [_mm(t_inv[q], a_32[q]) for q in qs]
        t_inv = [t_inv[q] + _mm(ta[q], t_inv[q]) for q in qs]
        a_64 = [jnp.where(in32, 0.0, a_ab[q]) for q in qs]
        ta = [_mm(t_inv[q], a_64[q]) for q in qs]
        t_inv = [t_inv[q] + _mm(ta[q], t_inv[q]) for q in qs]

        akv = [_mm(a_ak[q], v_s[q]) for q in qs]
        wu = [_mm(t_inv[q], jnp.concatenate([ah_s[q], akv[q]], axis=1)) for q in qs]
        bkd = [jnp.concatenate([stack(bh[q] * pc[q]), stack(kh[q] * pc[q]),
                                jnp.where(eye, pc[q], 0.0)], axis=0) for q in qs]

        m_st = [state_ref[q] for q in qs]
        u_s = [_mm(wu[q][:, :QUAD], m_st[q]) + wu[q][:, QUAD:] for q in qs]
        muv = [jnp.concatenate([m_st[q], u_s[q], v_s[q]], axis=0) for q in qs]
        y_s = [_mm(jnp.concatenate([rh_s[q], a_rb[q], a_rk[q]], axis=1), muv[q]) for q in qs]
        ys = []
        for q in qs:
            y = y_s[q][0:c_sz]
            for h in range(1, HEADS_PER_QUAD):
                y = y + y_s[q][h * c_sz:(h + 1) * c_sz]
            ys.append(y)
            state_ref[q] = _mm_tn(bkd[q], jnp.concatenate([u_s[q], v_s[q], m_st[q]], axis=0))
        return ys

    def post_chunk(c, ys, d):
        y = jnp.concatenate(ys, axis=0)
        yc = y - _mm(y, avg_bd)
        var = _mm(yc * yc, avg_bd)
        yn = yc * lax.rsqrt(var + LNX_EPS)
        yn = jnp.concatenate([yn[q * c_sz:(q + 1) * c_sz] for q in qs], axis=1)
        out = (yn * lng_ref[...] + lnb_ref[...] + d["bonus"]) * d["g"]
        o_ref[c * c_sz:(c + 1) * c_sz, :] = out.astype(o_ref.dtype)

    n_chunks = tb // c_sz
    nxt = prep_chunk(0)
    for c in range(n_chunks):
        cur = nxt
        if c + 1 < n_chunks:
            nxt = prep_chunk(c + 1)
        post_chunk(c, wkv_chunk(cur), cur)
    carry_ref[0:1, :] = p_ref[tb - 1:tb, :]


def _lane_window(start, stop):
    return (start // LANE) * LANE, -(-stop // LANE) * LANE


def _pad_rows_to_window(w, start, window):
    return jnp.pad(w, ((start - window[0], window[1] - start - w.shape[0]), (0, 0)))


def _rwkv_mix(p, mu, w0, w_lora_up, a0, a_lora_up, g_lora_up, k_k, k_a, r_k, lnx_g, lnx_b, *, tb):
    bsz, t, cols = p.shape
    rw = w0.shape[1]
    o1 = 3 * rw
    o2 = o1 + w_lora_up.shape[0]
    o3 = o2 + a_lora_up.shape[0]
    o4 = o3 + g_lora_up.shape[0]
    win_w, win_a, win_g = _lane_window(o1, o2), _lane_window(o2, o3), _lane_window(o3, o4)
    assert win_g[1] == cols
    wup = _pad_rows_to_window(w_lora_up, o1, win_w)
    aup = _pad_rows_to_window(a_lora_up, o2, win_a)
    gup = _pad_rows_to_window(g_lora_up, o3, win_g)
    row_spec = lambda w: pl.BlockSpec((1, w), lambda b, i: (0, 0))
    full = lambda a: pl.BlockSpec(a.shape, lambda b, i: (0, 0))
    return pl.pallas_call(
        functools.partial(_rwkv_mix_kernel, rw=rw, win_w=win_w, win_a=win_a, win_g=win_g),
        grid=(bsz, t // tb),
        in_specs=[
            pl.BlockSpec((None, tb, cols), lambda b, i: (b, i, 0)),
            row_spec(cols), row_spec(rw), full(wup), row_spec(rw), full(aup), full(gup),
            row_spec(rw), row_spec(rw), row_spec(rw), row_spec(rw), row_spec(rw),
        ],
        out_specs=pl.BlockSpec((None, tb, rw), lambda b, i: (b, i, 0)),
        out_shape=jax.ShapeDtypeStruct((bsz, t, rw), BF16),
        scratch_shapes=[pltpu.VMEM((8, cols), F32),
                        pltpu.VMEM((rw // QUAD, QUAD, QUAD), F32)],
        compiler_params=pltpu.CompilerParams(
            dimension_semantics=("parallel", "arbitrary"), vmem_limit_bytes=VMEM_LIMIT),
        name="rwkv_mix",
    )(p, mu, w0, wup, a0, aup, gup, k_k, k_a, r_k, lnx_g, lnx_b)


def _sgu_kernel(z_ref, lng_ref, lnb_ref, ws_ref, bs_ref, o_ref):
    tm, width = o_ref.shape
    u = z_ref[:, :width]
    v = z_ref[:, width:]
    mean = jnp.mean(v, axis=-1, keepdims=True)
    vc = v - mean
    var = jnp.mean(vc * vc, axis=-1, keepdims=True)
    vn = vc * lax.rsqrt(var + LN_EPS) * lng_ref[...] + lnb_ref[...]

    tr = lax.broadcasted_iota(jnp.int32, (SGU_CHUNK, SGU_CHUNK), 0)
    tc = lax.broadcasted_iota(jnp.int32, (SGU_CHUNK, SGU_CHUNK), 1)
    causal = tc <= tr
    bias = bs_ref[...]
    for g in range(width // SGU_GROUP):
        w_c = jnp.where(causal, ws_ref[g], 0.0)
        lanes = slice(g * SGU_GROUP, (g + 1) * SGU_GROUP)
        for n in range(tm // SGU_CHUNK):
            rows = slice(n * SGU_CHUNK, (n + 1) * SGU_CHUNK)
            mixed = _mm(w_c, vn[rows, lanes]) + bias[:, lanes]
            o_ref[rows, lanes] = (u[rows, lanes] * mixed).astype(o_ref.dtype)


def _sgu(z, ln_g, ln_b, w_s, b_full, *, tm):
    m, two_w = z.shape
    width = two_w // 2
    return pl.pallas_call(
        _sgu_kernel,
        grid=(m // tm,),
        in_specs=[
            pl.BlockSpec((tm, two_w), lambda i: (i, 0)),
            pl.BlockSpec((1, width), lambda i: (0, 0)),
            pl.BlockSpec((1, width), lambda i: (0, 0)),
            pl.BlockSpec(w_s.shape, lambda i: (0, 0, 0)),
            pl.BlockSpec(b_full.shape, lambda i: (0, 0)),
        ],
        out_specs=pl.BlockSpec((tm, width), lambda i: (i, 0)),
        out_shape=jax.ShapeDtypeStruct((m, width), BF16),
        compiler_params=pltpu.CompilerParams(
            dimension_semantics=("parallel",), vmem_limit_bytes=VMEM_LIMIT),
        name="sgu",
    )(z, ln_g, ln_b, w_s, b_full)


def _merge_kernel(ya_ref, yb_ref, gates_ref, x_ref, wa_ref, wb_ref, wo_ref, g2_ref, h_ref, n_ref):
    d = x_ref.shape[1]
    merged = (gates_ref[:, :d] * _mm(ya_ref[...], wa_ref[...])
              + gates_ref[:, d:] * _mm(yb_ref[...], wb_ref[...]))
    h = x_ref[...] + _mm(merged.astype(BF16), wo_ref[...])
    h_ref[...] = h
    n_ref[...] = _rmsnorm(h, g2_ref[...]).astype(BF16)


def _merge(ya, yb, gates, x, wa, wb, wo, g2, *, tm):
    m, d = x.shape
    rw = ya.shape[1]
    const = lambda a: pl.BlockSpec(a.shape, lambda i: (0, 0), pipeline_mode=pl.Buffered(1))
    return pl.pallas_call(
        _merge_kernel,
        grid=(m // tm,),
        in_specs=[
            pl.BlockSpec((tm, rw), lambda i: (i, 0)),
            pl.BlockSpec((tm, rw), lambda i: (i, 0)),
            pl.BlockSpec((tm, 2 * d), lambda i: (i, 0)),
            pl.BlockSpec((tm, d), lambda i: (i, 0)),
            const(wa), const(wb), const(wo),
            pl.BlockSpec((1, d), lambda i: (0, 0)),
        ],
        out_specs=[pl.BlockSpec((tm, d), lambda i: (i, 0)), pl.BlockSpec((tm, d), lambda i: (i, 0))],
        out_shape=[jax.ShapeDtypeStruct((m, d), F32), jax.ShapeDtypeStruct((m, d), BF16)],
        compiler_params=pltpu.CompilerParams(
            dimension_semantics=("parallel",), vmem_limit_bytes=VMEM_LIMIT),
        name="merge",
    )(ya, yb, gates, x, wa, wb, wo, g2)


def _ffn_kernel(n_ref, h_ref, wg_ref, wu_ref, wd_ref, gf_ref, o_ref, acc_ref):
    j = pl.program_id(1)

    @pl.when(j == 0)
    def _():
        acc_ref[...] = h_ref[...]

    n = n_ref[...]
    gate = _mm(n, wg_ref[...])
    up = _mm(n, wu_ref[...])
    act = (gate * jax.nn.sigmoid(gate) * up).astype(BF16)
    acc_ref[...] += _mm(act, wd_ref[...])

    @pl.when(j == pl.num_programs(1) - 1)
    def _():
        o_ref[...] = _rmsnorm(acc_ref[...], gf_ref[...])


def _ffn(n2, h1, wg, wu, wd, gf, *, tm, th):
    m, d = h1.shape
    hid = wg.shape[1]
    return pl.pallas_call(
        _ffn_kernel,
        grid=(m // tm, hid // th),
        in_specs=[
            pl.BlockSpec((tm, d), lambda i, j: (i, 0)),
            pl.BlockSpec((tm, d), lambda i, j: (i, 0)),
            pl.BlockSpec((d, th), lambda i, j: (0, j)),
            pl.BlockSpec((d, th), lambda i, j: (0, j)),
            pl.BlockSpec((th, d), lambda i, j: (j, 0)),
            pl.BlockSpec((1, d), lambda i, j: (0, 0)),
        ],
        out_specs=pl.BlockSpec((tm, d), lambda i, j: (i, 0)),
        out_shape=jax.ShapeDtypeStruct((m, d), F32),
        scratch_shapes=[pltpu.VMEM((tm, d), F32)],
        compiler_params=pltpu.CompilerParams(
            dimension_semantics=("parallel", "arbitrary"), vmem_limit_bytes=VMEM_LIMIT),
        name="ffn",
    )(n2, h1, wg, wu, wd, gf)


def _pad_cols(a, width):
    return jnp.pad(a, ((0, 0), (0, width - a.shape[1])))


def _block(h, norm_mix_g, w_in, shift_mu, w0, w_lora_up, a0, a_lora_up, g_lora_up, k_k, k_a, r_k,
           lnx_g, lnx_b, w_proj_rwkv, sgu_ln_g, sgu_ln_b, sgu_w, sgu_b, w_proj_sgu, w_out,
           norm_ffn_g, w_ffn_gate, w_ffn_up, w_ffn_down, g_last, *, bsz, seq):
    m, d = h.shape
    rw = w0.shape[0]
    sw = sgu_ln_g.shape[0]
    rcols = shift_mu.shape[0]
    zc = rcols + 2 * sw
    rcols_al = -(-rcols // LANE) * LANE
    row = lambda a: a.reshape(1, -1)

    w_in16 = w_in.astype(BF16)
    w_z = w_in16[:, rcols:zc]
    w_gates = w_in16[:, zc:]
    mu = _pad_cols(shift_mu[None, :], rcols_al)

    n1 = _rmsnorm_call(h, row(norm_mix_g), tm=512)
    p = _proj(n1, w_in16, rcols_al, tm=1024, tn=rcols_al // 2, epilogue="none")
    z = _proj(n1, w_z, 2 * sw, tm=1024, tn=1024, epilogue="gelu")
    gates = _proj(n1, w_gates, 2 * d, tm=1024, tn=1024, epilogue="sigmoid")

    ya = _rwkv_mix(p.reshape(bsz, seq, -1), mu, row(w0), w_lora_up, row(a0), a_lora_up, g_lora_up,
                   row(k_k), row(k_a), row(r_k), row(lnx_g), row(lnx_b), tb=256).reshape(m, rw)

    b_full = jnp.repeat(sgu_b.T, SGU_GROUP, axis=1)
    yb = _sgu(z, row(sgu_ln_g), row(sgu_ln_b), sgu_w, b_full, tm=512)

    h1, n2 = _merge(ya, yb, gates, h, w_proj_rwkv.astype(BF16), w_proj_sgu.astype(BF16),
                    w_out.astype(BF16), row(norm_ffn_g), tm=256)
    return _ffn(n2, h1, w_ffn_gate.astype(BF16), w_ffn_up.astype(BF16), w_ffn_down.astype(BF16),
                row(g_last), tm=512, th=512)


def kernel(x, norm_mix_g, w_in, shift_mu, w0, w_lora_up, a0, a_lora_up, g_lora_up, k_k, k_a, r_k,
           lnx_g, lnx_b, w_proj_rwkv, sgu_ln_g, sgu_ln_b, sgu_w, sgu_b, w_proj_sgu, w_out,
           norm_ffn_g, w_ffn_gate, w_ffn_up, w_ffn_down, norm_final_g):
    bsz, seq, d = x.shape
    depth = w_in.shape[0]
    assert depth == 1, "the final rmsnorm is fused into the (single) layer's channel mixer"
    h = x.reshape(bsz * seq, d)
    out = _block(h, norm_mix_g[0], w_in[0], shift_mu[0], w0[0], w_lora_up[0], a0[0], a_lora_up[0],
                 g_lora_up[0], k_k[0], k_a[0], r_k[0], lnx_g[0], lnx_b[0], w_proj_rwkv[0],
                 sgu_ln_g[0], sgu_ln_b[0], sgu_w[0], sgu_b[0], w_proj_sgu[0], w_out[0],
                 norm_ffn_g[0], w_ffn_gate[0], w_ffn_up[0], w_ffn_down[0], norm_final_g,
                 bsz=bsz, seq=seq)
    return out.reshape(bsz, seq, d)
```

```python
import functools

import jax
import jax.numpy as jnp
from jax import lax
from jax.experimental import pallas as pl
from jax.experimental.pallas import tpu as pltpu

F32 = jnp.float32
BF16 = jnp.bfloat16

RMS_EPS = 1e-6
LN_EPS = 1e-5
LNX_EPS = 64e-5

HEAD = 64
WKV_CHUNK = 64
QUAD = 256
HEADS_PER_QUAD = QUAD // HEAD
SGU_CHUNK = 128
SGU_GROUP = 128
LANE = 128
SUBLANE = 8

VMEM_LIMIT = 56 * 1024 * 1024


def _mm(a, b):
    return jnp.dot(a, b, preferred_element_type=F32)


def _mm_nt(a, b):
    return lax.dot_general(a, b, (((1,), (1,)), ((), ())), preferred_element_type=F32)


def _mm_tn(a, b):
    return lax.dot_general(a, b, (((0,), (0,)), ((), ())), preferred_element_type=F32)


def _rmsnorm(x, g):
    return x * lax.rsqrt(jnp.mean(x * x, axis=-1, keepdims=True) + RMS_EPS) * g


def _rmsnorm_kernel(x_ref, g_ref, o_ref):
    o_ref[...] = _rmsnorm(x_ref[...], g_ref[...]).astype(o_ref.dtype)


def _rmsnorm_call(x, g, *, tm):
    m, d = x.shape
    return pl.pallas_call(
        _rmsnorm_kernel,
        grid=(m // tm,),
        in_specs=[pl.BlockSpec((tm, d), lambda i: (i, 0)), pl.BlockSpec((1, d), lambda i: (0, 0))],
        out_specs=pl.BlockSpec((tm, d), lambda i: (i, 0)),
        out_shape=jax.ShapeDtypeStruct((m, d), BF16),
        compiler_params=pltpu.CompilerParams(dimension_semantics=("parallel",)),
        name="rmsnorm",
    )(x, g)


def _proj_kernel(n_ref, wt_ref, o_ref, w16_ref, *, epilogue, sub):
    @pl.when(pl.program_id(1) == 0)
    def _():
        w16_ref[...] = wt_ref[...].astype(BF16)

    for k in range(o_ref.shape[1] // sub):
        cols = slice(k * sub, (k + 1) * sub)
        y = _mm_nt(n_ref[...], w16_ref[cols, :])
        if epilogue == "gelu":
            y = 0.5 * y * (1.0 + lax.erf(y * (2.0 ** -0.5)))
        elif epilogue == "sigmoid":
            y = jax.nn.sigmoid(y)
        o_ref[:, cols] = y.astype(o_ref.dtype)


def _proj(n, w_t, row0, n_cols, *, tm, tn, epilogue, sub=256):
    m, d = n.shape
    return pl.pallas_call(
        functools.partial(_proj_kernel, epilogue=epilogue, sub=sub),
        grid=(n_cols // tn, m // tm),
        in_specs=[
            pl.BlockSpec((tm, d), lambda j, i: (i, 0)),
            pl.BlockSpec((pl.Element(tn), pl.Element(d)),
                         lambda j, i: (pl.multiple_of(row0 + j * tn, SUBLANE), 0)),
        ],
        out_specs=pl.BlockSpec((tm, tn), lambda j, i: (i, j)),
        out_shape=jax.ShapeDtypeStruct((m, n_cols), F32),
        scratch_shapes=[pltpu.VMEM((tn, d), BF16)],
        compiler_params=pltpu.CompilerParams(
            dimension_semantics=("parallel", "arbitrary"), vmem_limit_bytes=VMEM_LIMIT),
        name="proj_" + epilogue,
    )(n, w_t)


def _rwkv_mix_kernel(p_ref, mu_ref, w0_ref, wup_ref, a0_ref, aup_ref, gup_ref, kk_ref, ka_ref,
                     rk_ref, lng_ref, lnb_ref, *rest, rw, win_w, win_a, win_g, n_cast):
    cast_in, o_ref, cast_out = rest[:n_cast], rest[n_cast], rest[n_cast + 1:2 * n_cast + 1]
    carry_ref, state_ref = rest[2 * n_cast + 1:]
    for src_ref, dst_ref in zip(cast_in, cast_out):
        dst_ref[...] = src_ref[...].astype(dst_ref.dtype)

    tb = p_ref.shape[0]
    c_sz = WKV_CHUNK
    n_quads = rw // QUAD
    qs = range(n_quads)
    lanes = [slice(q * QUAD, (q + 1) * QUAD) for q in qs]
    rows = HEADS_PER_QUAD * c_sz
    assert rows == QUAD

    @pl.when(pl.program_id(1) == 0)
    def _():
        carry_ref[...] = jnp.zeros_like(carry_ref)
        state_ref[...] = jnp.zeros_like(state_ref)

    ri = lax.broadcasted_iota(jnp.int32, (rows, QUAD), 0)
    ci = lax.broadcasted_iota(jnp.int32, (rows, QUAD), 1)
    ones_bd = jnp.where((ri // HEAD) == (ci // HEAD), 1.0, 0.0).astype(F32)
    avg_bd = ones_bd * (1.0 / HEAD)
    head_mask = (ri // c_sz) == (ci // HEAD)
    same = (ri // c_sz) == (ci // c_sz)
    strict = same & (ci < ri)
    incl = same & (ci <= ri)
    leaf = (ri // 16) == (ci // 16)
    in32 = (ri // 32) == (ci // 32)
    eye = ri == ci
    eye_f = jnp.where(eye, 1.0, 0.0).astype(F32)
    tr = lax.broadcasted_iota(jnp.int32, (c_sz, c_sz), 0)
    tc = lax.broadcasted_iota(jnp.int32, (c_sz, c_sz), 1)
    tri = jnp.where(tc <= tr, 1.0, 0.0).astype(BF16)
    crow = lax.broadcasted_iota(jnp.int32, (c_sz, 1), 0)

    def head_sums(xs):
        lhs = jnp.concatenate([x[:, lanes[q]] for x in xs for q in qs], axis=0)
        out = _mm(lhs, ones_bd)
        return [jnp.concatenate([out[(i * n_quads + q) * c_sz:(i * n_quads + q + 1) * c_sz]
                                 for q in qs], axis=1) for i in range(len(xs))]

    def stack(x):
        return jnp.where(head_mask, jnp.concatenate([x] * HEADS_PER_QUAD, axis=0), 0.0)

    def expand(x):
        return jnp.concatenate([x, x], axis=1)

    def prep_chunk(c):
        r0 = c * c_sz
        p = p_ref[r0:r0 + c_sz, :]
        before = carry_ref[0:1, :] if c == 0 else p_ref[r0 - 1:r0, :]
        prev = jnp.where(crow == 0, before, pltpu.roll(p, 1, 0))
        sh = p + (prev - p) * mu_ref[...]
        r = sh[:, 0:rw]
        k = sh[:, rw:2 * rw]
        v = sh[:, 2 * rw:3 * rw]
        xw = sh[:, win_w[0]:win_w[1]]
        xa = sh[:, win_a[0]:win_a[1]]
        xg = sh[:, win_g[0]:win_g[1]]

        wl = w0_ref[...] + _mm(jnp.tanh(xw), wup_ref[...])
        lw = (-(2.718281828459045 ** -0.5)) * jax.nn.sigmoid(wl)
        a = jax.nn.sigmoid(a0_ref[...] + _mm(xa, aup_ref[...]))
        g = _mm(jax.nn.sigmoid(xg), gup_ref[...])

        kk = k * kk_ref[...]
        k2 = k * (1.0 + (a - 1.0) * ka_ref[...])
        kk_ss, rk_sum = head_sums([kk * kk, r * k2 * rk_ref[...]])
        kk = kk * jnp.minimum(lax.rsqrt(kk_ss), 1e12)
        bonus = rk_sum * v

        h1 = lw.astype(BF16)
        r1 = lw - h1.astype(F32)
        h2 = r1.astype(BF16)
        h3 = (r1 - h2.astype(F32)).astype(BF16)
        cum = _mm(tri, h1) + _mm(tri, h2) + _mm(tri, h3)
        ec = jnp.exp(cum)
        eci = jnp.exp(-cum)
        ec_prev = jnp.where(crow == 0, 1.0, pltpu.roll(ec, 1, 0))
        return dict(ah=-kk * ec_prev, rh=r * ec, bh=kk * a * eci, kh=k2 * eci, v=v,
                    pc=ec[c_sz - 1:c_sz, :], bonus=bonus, g=g)

    def wkv_chunk(d):
        ah = [d["ah"][:, lanes[q]] for q in qs]
        rh = [d["rh"][:, lanes[q]] for q in qs]
        bh = [d["bh"][:, lanes[q]] for q in qs]
        kh = [d["kh"][:, lanes[q]] for q in qs]
        vv = [d["v"][:, lanes[q]] for q in qs]
        pc = [d["pc"][:, lanes[q]] for q in qs]

        ah_s = [stack(x) for x in ah]
        rh_s = [stack(x) for x in rh]
        v_s = [stack(x) for x in vv]
        s = [_mm_nt(jnp.concatenate([ah_s[q], rh_s[q]], axis=0),
                    jnp.concatenate([bh[q], bh[q], kh[q], kh[q]], axis=0)) for q in qs]
        a_ab = [jnp.where(strict, expand(s[q][:rows, :QUAD // 2]), 0.0) for q in qs]
        a_ak = [jnp.where(strict, expand(s[q][:rows, QUAD // 2:]), 0.0) for q in qs]
        a_rb = [jnp.where(incl, expand(s[q][rows:, :QUAD // 2]), 0.0) for q in qs]
        a_rk = [jnp.where(incl, expand(s[q][rows:, QUAD // 2:]), 0.0) for q in qs]

        a_d = [jnp.where(leaf, a_ab[q], 0.0) for q in qs]
        t_inv = [eye_f + a_d[q] for q in qs]
        pw = [_mm(a_d[q], a_d[q]) for q in qs]
        t_inv = [t_inv[q] + _mm(pw[q], t_inv[q]) for q in qs]
        pw = [_mm(pw[q], pw[q]) for q in qs]
        t_inv = [t_inv[q] + _mm(pw[q], t_inv[q]) for q in qs]
        pw = [_mm(pw[q], pw[q]) for q in qs]
        t_inv = [t_inv[q] + _mm(pw[q], t_inv[q]) for q in qs]
        a_32 = [jnp.where(in32 & jnp.logical_not(leaf), a_ab[q], 0.0) for q in qs]
        ta = [_mm(t_inv[q], a_32[q]) for q in qs]
        t_inv = [t_inv[q] + _mm(ta[q], t_inv[q]) for q in qs]
        a_64 = [jnp.where(in32, 0.0, a_ab[q]) for q in qs]
        ta = [_mm(t_inv[q], a_64[q]) for q in qs]
        t_inv = [t_inv[q] + _mm(ta[q], t_inv[q]) for q in qs]

        akv = [_mm(a_ak[q], v_s[q]) for q in qs]
        wu = [_mm(t_inv[q], jnp.concatenate([ah_s[q], akv[q]], axis=1)) for q in qs]
        bkd = [jnp.concatenate([stack(bh[q] * pc[q]), stack(kh[q] * pc[q]),
                                jnp.where(eye, pc[q], 0.0)], axis=0) for q in qs]

        m_st = [state_ref[q] for q in qs]
        u_s = [_mm(wu[q][:, :QUAD], m_st[q]) + wu[q][:, QUAD:] for q in qs]
        muv = [jnp.concatenate([m_st[q], u_s[q], v_s[q]], axis=0) for q in qs]
        y_s = [_mm(jnp.concatenate([rh_s[q], a_rb[q], a_rk[q]], axis=1), muv[q]) for q in qs]
        ys = []
        for q in qs:
            y = y_s[q][0:c_sz]
            for h in range(1, HEADS_PER_QUAD):
                y = y + y_s[q][h * c_sz:(h + 1) * c_sz]
            ys.append(y)
            state_ref[q] = _mm_tn(bkd[q], jnp.concatenate([u_s[q], v_s[q], m_st[q]], axis=0))
        return ys

    def post_chunk(c, ys, d):
        y = jnp.concatenate(ys, axis=0)
        yc = y - _mm(y, avg_bd)
        var = _mm(yc * yc, avg_bd)
        yn = yc * lax.rsqrt(var + LNX_EPS)
        yn = jnp.concatenate([yn[q * c_sz:(q + 1) * c_sz] for q in qs], axis=1)
        out = (yn * lng_ref[...] + lnb_ref[...] + d["bonus"]) * d["g"]
        o_ref[c * c_sz:(c + 1) * c_sz, :] = out.astype(o_ref.dtype)

    n_chunks = tb // c_sz
    nxt = prep_chunk(0)
    for c in range(n_chunks):
        cur = nxt
        if c + 1 < n_chunks:
            nxt = prep_chunk(c + 1)
        post_chunk(c, wkv_chunk(cur), cur)
    carry_ref[0:1, :] = p_ref[tb - 1:tb, :]


def _lane_window(start, stop):
    return (start // LANE) * LANE, -(-stop // LANE) * LANE


def _pad_rows_to_window(w, start, window):
    return jnp.pad(w, ((start - window[0], window[1] - start - w.shape[0]), (0, 0)))


def _rwkv_mix(p, mu, w0, w_lora_up, a0, a_lora_up, g_lora_up, k_k, k_a, r_k, lnx_g, lnx_b, cast, *,
              tb):
    bsz, t, cols = p.shape
    rw = w0.shape[1]
    o1 = 3 * rw
    o2 = o1 + w_lora_up.shape[0]
    o3 = o2 + a_lora_up.shape[0]
    o4 = o3 + g_lora_up.shape[0]
    win_w, win_a, win_g = _lane_window(o1, o2), _lane_window(o2, o3), _lane_window(o3, o4)
    assert win_g[1] == cols
    wup = _pad_rows_to_window(w_lora_up, o1, win_w)
    aup = _pad_rows_to_window(a_lora_up, o2, win_a)
    gup = _pad_rows_to_window(g_lora_up, o3, win_g)
    row_spec = lambda w: pl.BlockSpec((1, w), lambda b, i: (0, 0))
    full = lambda a: pl.BlockSpec(a.shape, lambda b, i: (0, 0))
    nt = t // tb
    steps = bsz * nt
    slab = lambda a: pl.BlockSpec((a.shape[0] // steps, a.shape[1]), lambda b, i: (b * nt + i, 0))
    for a in cast:
        assert a.shape[0] % (steps * 2 * SUBLANE) == 0, a.shape
    outs = pl.pallas_call(
        functools.partial(_rwkv_mix_kernel, rw=rw, win_w=win_w, win_a=win_a, win_g=win_g,
                          n_cast=len(cast)),
        grid=(bsz, nt),
        in_specs=[
            pl.BlockSpec((None, tb, cols), lambda b, i: (b, i, 0)),
            row_spec(cols), row_spec(rw), full(wup), row_spec(rw), full(aup), full(gup),
            row_spec(rw), row_spec(rw), row_spec(rw), row_spec(rw), row_spec(rw),
        ] + [slab(a) for a in cast],
        out_specs=[pl.BlockSpec((None, tb, rw), lambda b, i: (b, i, 0))] + [slab(a) for a in cast],
        out_shape=[jax.ShapeDtypeStruct((bsz, t, rw), BF16)]
        + [jax.ShapeDtypeStruct(a.shape, BF16) for a in cast],
        scratch_shapes=[pltpu.VMEM((8, cols), F32),
                        pltpu.VMEM((rw // QUAD, QUAD, QUAD), F32)],
        compiler_params=pltpu.CompilerParams(
            dimension_semantics=("parallel", "arbitrary"), vmem_limit_bytes=VMEM_LIMIT),
        name="rwkv_mix",
    )(p, mu, w0, wup, a0, aup, gup, k_k, k_a, r_k, lnx_g, lnx_b, *cast)
    return outs[0], outs[1:]


def _sgu_kernel(z_ref, lng_ref, lnb_ref, ws_ref, bs_ref, o_ref):
    tm, width = o_ref.shape
    u = z_ref[:, :width]
    v = z_ref[:, width:]
    mean = jnp.mean(v, axis=-1, keepdims=True)
    vc = v - mean
    var = jnp.mean(vc * vc, axis=-1, keepdims=True)
    vn = vc * lax.rsqrt(var + LN_EPS) * lng_ref[...] + lnb_ref[...]

    tr = lax.broadcasted_iota(jnp.int32, (SGU_CHUNK, SGU_CHUNK), 0)
    tc = lax.broadcasted_iota(jnp.int32, (SGU_CHUNK, SGU_CHUNK), 1)
    causal = tc <= tr
    bias = bs_ref[...]
    for g in range(width // SGU_GROUP):
        w_c = jnp.where(causal, ws_ref[g], 0.0)
        lanes = slice(g * SGU_GROUP, (g + 1) * SGU_GROUP)
        for n in range(tm // SGU_CHUNK):
            rows = slice(n * SGU_CHUNK, (n + 1) * SGU_CHUNK)
            mixed = _mm(w_c, vn[rows, lanes]) + bias[:, lanes]
            o_ref[rows, lanes] = (u[rows, lanes] * mixed).astype(o_ref.dtype)


def _sgu(z, ln_g, ln_b, w_s, b_full, *, tm):
    m, two_w = z.shape
    width = two_w // 2
    return pl.pallas_call(
        _sgu_kernel,
        grid=(m // tm,),
        in_specs=[
            pl.BlockSpec((tm, two_w), lambda i: (i, 0)),
            pl.BlockSpec((1, width), lambda i: (0, 0)),
            pl.BlockSpec((1, width), lambda i: (0, 0)),
            pl.BlockSpec(w_s.shape, lambda i: (0, 0, 0)),
            pl.BlockSpec(b_full.shape, lambda i: (0, 0)),
        ],
        out_specs=pl.BlockSpec((tm, width), lambda i: (i, 0)),
        out_shape=jax.ShapeDtypeStruct((m, width), BF16),
        compiler_params=pltpu.CompilerParams(
            dimension_semantics=("parallel",), vmem_limit_bytes=VMEM_LIMIT),
        name="sgu",
    )(z, ln_g, ln_b, w_s, b_full)


def _merge_kernel(ya_ref, yb_ref, gates_ref, x_ref, wa_ref, wb_ref, wo_ref, g2_ref, h_ref, n_ref):
    d = x_ref.shape[1]
    merged = (gates_ref[:, :d] * _mm(ya_ref[...], wa_ref[...])
              + gates_ref[:, d:] * _mm(yb_ref[...], wb_ref[...]))
    h = x_ref[...] + _mm(merged.astype(BF16), wo_ref[...])
    h_ref[...] = h
    n_ref[...] = _rmsnorm(h, g2_ref[...]).astype(BF16)


def _merge(ya, yb, gates, x, wa, wb, wo, g2, *, tm):
    m, d = x.shape
    rw = ya.shape[1]
    const = lambda a: pl.BlockSpec(a.shape, lambda i: (0, 0), pipeline_mode=pl.Buffered(1))
    return pl.pallas_call(
        _merge_kernel,
        grid=(m // tm,),
        in_specs=[
            pl.BlockSpec((tm, rw), lambda i: (i, 0)),
            pl.BlockSpec((tm, rw), lambda i: (i, 0)),
            pl.BlockSpec((tm, 2 * d), lambda i: (i, 0)),
            pl.BlockSpec((tm, d), lambda i: (i, 0)),
            const(wa), const(wb), const(wo),
            pl.BlockSpec((1, d), lambda i: (0, 0)),
        ],
        out_specs=[pl.BlockSpec((tm, d), lambda i: (i, 0)), pl.BlockSpec((tm, d), lambda i: (i, 0))],
        out_shape=[jax.ShapeDtypeStruct((m, d), F32), jax.ShapeDtypeStruct((m, d), BF16)],
        compiler_params=pltpu.CompilerParams(
            dimension_semantics=("parallel",), vmem_limit_bytes=VMEM_LIMIT),
        name="merge",
    )(ya, yb, gates, x, wa, wb, wo, g2)


def _ffn_kernel(n_ref, h_ref, wg_ref, wu_ref, wd_ref, gf_ref, o_ref, acc_ref):
    j = pl.program_id(1)

    @pl.when(j == 0)
    def _():
        acc_ref[...] = h_ref[...]

    n = n_ref[...]
    gate = _mm(n, wg_ref[...])
    up = _mm(n, wu_ref[...])
    act = (gate * jax.nn.sigmoid(gate) * up).astype(BF16)
    acc_ref[...] += _mm(act, wd_ref[...])

    @pl.when(j == pl.num_programs(1) - 1)
    def _():
        o_ref[...] = _rmsnorm(acc_ref[...], gf_ref[...])


def _ffn(n2, h1, wg, wu, wd, gf, *, tm, th):
    m, d = h1.shape
    hid = wg.shape[1]
    return pl.pallas_call(
        _ffn_kernel,
        grid=(m // tm, hid // th),
        in_specs=[
            pl.BlockSpec((tm, d), lambda i, j: (i, 0)),
            pl.BlockSpec((tm, d), lambda i, j: (i, 0)),
            pl.BlockSpec((d, th), lambda i, j: (0, j)),
            pl.BlockSpec((d, th), lambda i, j: (0, j)),
            pl.BlockSpec((th, d), lambda i, j: (j, 0)),
            pl.BlockSpec((1, d), lambda i, j: (0, 0)),
        ],
        out_specs=pl.BlockSpec((tm, d), lambda i, j: (i, 0)),
        out_shape=jax.ShapeDtypeStruct((m, d), F32),
        scratch_shapes=[pltpu.VMEM((tm, d), F32)],
        compiler_params=pltpu.CompilerParams(
            dimension_semantics=("parallel", "arbitrary"), vmem_limit_bytes=VMEM_LIMIT),
        name="ffn",
    )(n2, h1, wg, wu, wd, gf)


def _pad_cols(a, width):
    return jnp.pad(a, ((0, 0), (0, width - a.shape[1])))


def _block(h, norm_mix_g, w_in, shift_mu, w0, w_lora_up, a0, a_lora_up, g_lora_up, k_k, k_a, r_k,
           lnx_g, lnx_b, w_proj_rwkv, sgu_ln_g, sgu_ln_b, sgu_w, sgu_b, w_proj_sgu, w_out,
           norm_ffn_g, w_ffn_gate, w_ffn_up, w_ffn_down, g_last, *, bsz, seq):
    m, d = h.shape
    rw = w0.shape[0]
    sw = sgu_ln_g.shape[0]
    rcols = shift_mu.shape[0]
    zc = rcols + 2 * sw
    rcols_al = -(-rcols // LANE) * LANE
    row = lambda a: a.reshape(1, -1)

    w_t = w_in.T
    mu = _pad_cols(shift_mu[None, :], rcols_al)

    n1 = _rmsnorm_call(h, row(norm_mix_g), tm=512)
    p = _proj(n1, w_t, 0, rcols_al, tm=2048, tn=512, epilogue="none")
    z = _proj(n1, w_t, rcols, 2 * sw, tm=1024, tn=1024, epilogue="gelu")
    gates = _proj(n1, w_t, zc, 2 * d, tm=1024, tn=1024, epilogue="sigmoid")

    ya, (wa, wb, wo, wg, wu, wd) = _rwkv_mix(
        p.reshape(bsz, seq, -1), mu, row(w0), w_lora_up, row(a0), a_lora_up, g_lora_up,
        row(k_k), row(k_a), row(r_k), row(lnx_g), row(lnx_b),
        (w_proj_rwkv, w_proj_sgu, w_out, w_ffn_gate, w_ffn_up, w_ffn_down), tb=256)

    b_full = jnp.repeat(sgu_b.T, SGU_GROUP, axis=1)
    yb = _sgu(z, row(sgu_ln_g), row(sgu_ln_b), sgu_w, b_full, tm=512)

    h1, n2 = _merge(ya.reshape(m, rw), yb, gates, h, wa, wb, wo, row(norm_ffn_g), tm=256)
    return _ffn(n2, h1, wg, wu, wd, row(g_last), tm=512, th=512)


def kernel(x, norm_mix_g, w_in, shift_mu, w0, w_lora_up, a0, a_lora_up, g_lora_up, k_k, k_a, r_k,
           lnx_g, lnx_b, w_proj_rwkv, sgu_ln_g, sgu_ln_b, sgu_w, sgu_b, w_proj_sgu, w_out,
           norm_ffn_g, w_ffn_gate, w_ffn_up, w_ffn_down, norm_final_g):
    bsz, seq, d = x.shape
    depth = w_in.shape[0]
    assert depth == 1, "the final rmsnorm is fused into the (single) layer's channel mixer"
    h = x.reshape(bsz * seq, d)
    out = _block(h, norm_mix_g[0], w_in[0], shift_mu[0], w0[0], w_lora_up[0], a0[0], a_lora_up[0],
                 g_lora_up[0], k_k[0], k_a[0], r_k[0], lnx_g[0], lnx_b[0], w_proj_rwkv[0],
                 sgu_ln_g[0], sgu_ln_b[0], sgu_w[0], sgu_b[0], w_proj_sgu[0], w_out[0],
                 norm_ffn_g[0], w_ffn_gate[0], w_ffn_up[0], w_ffn_down[0], norm_final_g,
                 bsz=bsz, seq=seq)
    return out.reshape(bsz, seq, d)
```

```python
import functools

import jax
import jax.numpy as jnp
from jax import lax
from jax.experimental import pallas as pl
from jax.experimental.pallas import tpu as pltpu

F32 = jnp.float32
BF16 = jnp.bfloat16

RMS_EPS = 1e-6
LN_EPS = 1e-5
LNX_EPS = 64e-5

HEAD = 64
WKV_CHUNK = 64
QUAD = 256
HEADS_PER_QUAD = QUAD // HEAD
SGU_CHUNK = 128
SGU_GROUP = 128
LANE = 128
SUBLANE = 8

VMEM_LIMIT = 60 * 1024 * 1024


def _mm(a, b):
    return jnp.dot(a, b, preferred_element_type=F32)


def _mm_nt(a, b):
    return lax.dot_general(a, b, (((1,), (1,)), ((), ())), preferred_element_type=F32)


def _mm_tn(a, b):
    return lax.dot_general(a, b, (((0,), (0,)), ((), ())), preferred_element_type=F32)


def _rmsnorm(x, g):
    return x * lax.rsqrt(jnp.mean(x * x, axis=-1, keepdims=True) + RMS_EPS) * g


def _rmsnorm_kernel(x_ref, g_ref, o_ref):
    o_ref[...] = _rmsnorm(x_ref[...], g_ref[...]).astype(o_ref.dtype)


def _rmsnorm_call(x, g, *, tm):
    m, d = x.shape
    return pl.pallas_call(
        _rmsnorm_kernel,
        grid=(m // tm,),
        in_specs=[pl.BlockSpec((tm, d), lambda i: (i, 0)), pl.BlockSpec((1, d), lambda i: (0, 0))],
        out_specs=pl.BlockSpec((tm, d), lambda i: (i, 0)),
        out_shape=jax.ShapeDtypeStruct((m, d), BF16),
        compiler_params=pltpu.CompilerParams(dimension_semantics=("parallel",)),
        name="rmsnorm",
    )(x, g)


def _proj_kernel(n_ref, wt_ref, o_ref, w16_ref, *, epilogue, sub):
    @pl.when(pl.program_id(1) == 0)
    def _():
        w16_ref[...] = wt_ref[...].astype(BF16)

    for k in range(o_ref.shape[1] // sub):
        cols = slice(k * sub, (k + 1) * sub)
        y = _mm_nt(n_ref[...], w16_ref[cols, :])
        if epilogue == "gelu":
            y = 0.5 * y * (1.0 + lax.erf(y * (2.0 ** -0.5)))
        elif epilogue == "sigmoid":
            y = jax.nn.sigmoid(y)
        o_ref[:, cols] = y.astype(o_ref.dtype)


def _proj(n, w_t, row0, n_cols, *, tm, tn, epilogue, sub=256):
    m, d = n.shape
    return pl.pallas_call(
        functools.partial(_proj_kernel, epilogue=epilogue, sub=sub),
        grid=(n_cols // tn, m // tm),
        in_specs=[
            pl.BlockSpec((tm, d), lambda j, i: (i, 0)),
            pl.BlockSpec((pl.Element(tn), pl.Element(d)),
                         lambda j, i: (pl.multiple_of(row0 + j * tn, SUBLANE), 0)),
        ],
        out_specs=pl.BlockSpec((tm, tn), lambda j, i: (i, j)),
        out_shape=jax.ShapeDtypeStruct((m, n_cols), F32),
        scratch_shapes=[pltpu.VMEM((tn, d), BF16)],
        compiler_params=pltpu.CompilerParams(
            dimension_semantics=("parallel", "arbitrary"), vmem_limit_bytes=VMEM_LIMIT),
        name="proj_" + epilogue,
    )(n, w_t)


def _rwkv_mix_kernel(p_ref, mu_ref, w0_ref, wup_ref, a0_ref, aup_ref, gup_ref, kk_ref, ka_ref,
                     rk_ref, lng_ref, lnb_ref, *rest, rw, win_w, win_a, win_g, n_cast):
    cast_in, o_ref, cast_out = rest[:n_cast], rest[n_cast], rest[n_cast + 1:2 * n_cast + 1]
    carry_ref, state_ref = rest[2 * n_cast + 1:]
    for src_ref, dst_ref in zip(cast_in, cast_out):
        dst_ref[...] = src_ref[...].astype(dst_ref.dtype)

    tb = p_ref.shape[0]
    c_sz = WKV_CHUNK
    n_quads = rw // QUAD
    qs = range(n_quads)
    lanes = [slice(q * QUAD, (q + 1) * QUAD) for q in qs]
    rows = HEADS_PER_QUAD * c_sz
    assert rows == QUAD

    @pl.when(pl.program_id(1) == 0)
    def _():
        carry_ref[...] = jnp.zeros_like(carry_ref)
        state_ref[...] = jnp.zeros_like(state_ref)

    ri = lax.broadcasted_iota(jnp.int32, (rows, QUAD), 0)
    ci = lax.broadcasted_iota(jnp.int32, (rows, QUAD), 1)
    ones_bd = jnp.where((ri // HEAD) == (ci // HEAD), 1.0, 0.0).astype(F32)
    avg_bd = ones_bd * (1.0 / HEAD)
    head_mask = (ri // c_sz) == (ci // HEAD)
    same = (ri // c_sz) == (ci // c_sz)
    strict = same & (ci < ri)
    incl = same & (ci <= ri)
    leaf = (ri // 16) == (ci // 16)
    in32 = (ri // 32) == (ci // 32)
    eye = ri == ci
    eye_f = jnp.where(eye, 1.0, 0.0).astype(F32)
    tr = lax.broadcasted_iota(jnp.int32, (c_sz, c_sz), 0)
    tc = lax.broadcasted_iota(jnp.int32, (c_sz, c_sz), 1)
    tri = jnp.where(tc <= tr, 1.0, 0.0).astype(BF16)
    crow = lax.broadcasted_iota(jnp.int32, (c_sz, 1), 0)

    def head_sums(xs):
        lhs = jnp.concatenate([x[:, lanes[q]] for x in xs for q in qs], axis=0)
        out = _mm(lhs, ones_bd)
        return [jnp.concatenate([out[(i * n_quads + q) * c_sz:(i * n_quads + q + 1) * c_sz]
                                 for q in qs], axis=1) for i in range(len(xs))]

    def stack(x):
        return jnp.where(head_mask, jnp.concatenate([x] * HEADS_PER_QUAD, axis=0), 0.0)

    def expand(x):
        return jnp.concatenate([x, x], axis=1)

    def prep_chunk(c):
        r0 = c * c_sz
        p = p_ref[r0:r0 + c_sz, :]
        before = carry_ref[0:1, :] if c == 0 else p_ref[r0 - 1:r0, :]
        prev = jnp.where(crow == 0, before, pltpu.roll(p, 1, 0))
        sh = p + (prev - p) * mu_ref[...]
        r = sh[:, 0:rw]
        k = sh[:, rw:2 * rw]
        v = sh[:, 2 * rw:3 * rw]
        xw = sh[:, win_w[0]:win_w[1]]
        xa = sh[:, win_a[0]:win_a[1]]
        xg = sh[:, win_g[0]:win_g[1]]

        wl = w0_ref[...] + _mm(jnp.tanh(xw), wup_ref[...])
        lw = (-(2.718281828459045 ** -0.5)) * jax.nn.sigmoid(wl)
        a = jax.nn.sigmoid(a0_ref[...] + _mm(xa, aup_ref[...]))
        g = _mm(jax.nn.sigmoid(xg), gup_ref[...])

        kk = k * kk_ref[...]
        k2 = k * (1.0 + (a - 1.0) * ka_ref[...])
        kk_ss, rk_sum = head_sums([kk * kk, r * k2 * rk_ref[...]])
        kk = kk * jnp.minimum(lax.rsqrt(kk_ss), 1e12)
        bonus = rk_sum * v

        h1 = lw.astype(BF16)
        r1 = lw - h1.astype(F32)
        h2 = r1.astype(BF16)
        h3 = (r1 - h2.astype(F32)).astype(BF16)
        cum = _mm(tri, h1) + _mm(tri, h2) + _mm(tri, h3)
        ec = jnp.exp(cum)
        eci = jnp.exp(-cum)
        ec_prev = jnp.where(crow == 0, 1.0, pltpu.roll(ec, 1, 0))
        return dict(ah=-kk * ec_prev, rh=r * ec, bh=kk * a * eci, kh=k2 * eci, v=v,
                    pc=ec[c_sz - 1:c_sz, :], bonus=bonus, g=g)

    def wkv_chunk(d):
        ah = [d["ah"][:, lanes[q]] for q in qs]
        rh = [d["rh"][:, lanes[q]] for q in qs]
        bh = [d["bh"][:, lanes[q]] for q in qs]
        kh = [d["kh"][:, lanes[q]] for q in qs]
        vv = [d["v"][:, lanes[q]] for q in qs]
        pc = [d["pc"][:, lanes[q]] for q in qs]

        ah_s = [stack(x) for x in ah]
        rh_s = [stack(x) for x in rh]
        v_s = [stack(x) for x in vv]
        s = [_mm_nt(jnp.concatenate([ah_s[q], rh_s[q]], axis=0),
                    jnp.concatenate([bh[q], bh[q], kh[q], kh[q]], axis=0)) for q in qs]
        a_ab = [jnp.where(strict, expand(s[q][:rows, :QUAD // 2]), 0.0) for q in qs]
        a_ak = [jnp.where(strict, expand(s[q][:rows, QUAD // 2:]), 0.0) for q in qs]
        a_rb = [jnp.where(incl, expand(s[q][rows:, :QUAD // 2]), 0.0) for q in qs]
        a_rk = [jnp.where(incl, expand(s[q][rows:, QUAD // 2:]), 0.0) for q in qs]

        a_d = [jnp.where(leaf, a_ab[q], 0.0) for q in qs]
        t_inv = [eye_f + a_d[q] for q in qs]
        pw = [_mm(a_d[q], a_d[q]) for q in qs]
        t_inv = [t_inv[q] + _mm(pw[q], t_inv[q]) for q in qs]
        pw = [_mm(pw[q], pw[q]) for q in qs]
        t_inv = [t_inv[q] + _mm(pw[q], t_inv[q]) for q in qs]
        pw = [_mm(pw[q], pw[q]) for q in qs]
        t_inv = [t_inv[q] + _mm(pw[q], t_inv[q]) for q in qs]
        a_32 = [jnp.where(in32 & jnp.logical_not(leaf), a_ab[q], 0.0) for q in qs]
        ta = [_mm(t_inv[q], a_32[q]) for q in qs]
        t_inv = [t_inv[q] + _mm(ta[q], t_inv[q]) for q in qs]
        a_64 = [jnp.where(in32, 0.0, a_ab[q]) for q in qs]
        ta = [_mm(t_inv[q], a_64[q]) for q in qs]
        t_inv = [t_inv[q] + _mm(ta[q], t_inv[q]) for q in qs]

        akv = [_mm(a_ak[q], v_s[q]) for q in qs]
        wu = [_mm(t_inv[q], jnp.concatenate([ah_s[q], akv[q]], axis=1)) for q in qs]
        bk = [jnp.concatenate([stack(bh[q] * pc[q]), stack(kh[q] * pc[q])], axis=0) for q in qs]
        pc_col = [jnp.sum(jnp.where(eye, pc[q], 0.0), axis=1, keepdims=True) for q in qs]
        return dict(rh_s=rh_s, v_s=v_s, a_rb=a_rb, a_rk=a_rk, wu=wu, bk=bk, pc_col=pc_col)

    def wkv_state(w):
        rh_s, v_s, a_rb, a_rk = w["rh_s"], w["v_s"], w["a_rb"], w["a_rk"]
        wu, bk, pc_col = w["wu"], w["bk"], w["pc_col"]
        m_st = [state_ref[q] for q in qs]
        u_s = [_mm(wu[q][:, :QUAD], m_st[q]) + wu[q][:, QUAD:] for q in qs]
        muv = [jnp.concatenate([m_st[q], u_s[q], v_s[q]], axis=0) for q in qs]
        y_s = [_mm(jnp.concatenate([rh_s[q], a_rb[q], a_rk[q]], axis=1), muv[q]) for q in qs]
        ys = []
        for q in qs:
            y = y_s[q][0:c_sz]
            for h in range(1, HEADS_PER_QUAD):
                y = y + y_s[q][h * c_sz:(h + 1) * c_sz]
            ys.append(y)
            state_ref[q] = (pc_col[q] * m_st[q]
                            + _mm_tn(bk[q], jnp.concatenate([u_s[q], v_s[q]], axis=0)))
        return ys

    def post_chunk(c, ys, d):
        y = jnp.concatenate(ys, axis=0)
        yc = y - _mm(y, avg_bd)
        var = _mm(yc * yc, avg_bd)
        yn = yc * lax.rsqrt(var + LNX_EPS)
        yn = jnp.concatenate([yn[q * c_sz:(q + 1) * c_sz] for q in qs], axis=1)
        out = (yn * lng_ref[...] + lnb_ref[...] + d["bonus"]) * d["g"]
        o_ref[c * c_sz:(c + 1) * c_sz, :] = out.astype(o_ref.dtype)

    n_chunks = tb // c_sz
    nxt = prep_chunk(0)
    nxt_w = wkv_chunk(nxt)
    for c in range(n_chunks):
        cur, cur_w = nxt, nxt_w
        if c + 1 < n_chunks:
            nxt = prep_chunk(c + 1)
            nxt_w = wkv_chunk(nxt)
        post_chunk(c, wkv_state(cur_w), cur)
    carry_ref[0:1, :] = p_ref[tb - 1:tb, :]


def _lane_window(start, stop):
    return (start // LANE) * LANE, -(-stop // LANE) * LANE


def _pad_rows_to_window(w, start, window):
    return jnp.pad(w, ((start - window[0], window[1] - start - w.shape[0]), (0, 0)))


def _rwkv_mix(p, mu, w0, w_lora_up, a0, a_lora_up, g_lora_up, k_k, k_a, r_k, lnx_g, lnx_b, cast, *,
              tb):
    bsz, t, cols = p.shape
    rw = w0.shape[1]
    o1 = 3 * rw
    o2 = o1 + w_lora_up.shape[0]
    o3 = o2 + a_lora_up.shape[0]
    o4 = o3 + g_lora_up.shape[0]
    win_w, win_a, win_g = _lane_window(o1, o2), _lane_window(o2, o3), _lane_window(o3, o4)
    assert win_g[1] == cols
    wup = _pad_rows_to_window(w_lora_up, o1, win_w)
    aup = _pad_rows_to_window(a_lora_up, o2, win_a)
    gup = _pad_rows_to_window(g_lora_up, o3, win_g)
    row_spec = lambda w: pl.BlockSpec((1, w), lambda b, i: (0, 0))
    full = lambda a: pl.BlockSpec(a.shape, lambda b, i: (0, 0))
    nt = t // tb
    steps = bsz * nt
    slab = lambda a: pl.BlockSpec((a.shape[0] // steps, a.shape[1]), lambda b, i: (b * nt + i, 0))
    for a in cast:
        assert a.shape[0] % (steps * 2 * SUBLANE) == 0, a.shape
    outs = pl.pallas_call(
        functools.partial(_rwkv_mix_kernel, rw=rw, win_w=win_w, win_a=win_a, win_g=win_g,
                          n_cast=len(cast)),
        grid=(bsz, nt),
        in_specs=[
            pl.BlockSpec((None, tb, cols), lambda b, i: (b, i, 0)),
            row_spec(cols), row_spec(rw), full(wup), row_spec(rw), full(aup), full(gup),
            row_spec(rw), row_spec(rw), row_spec(rw), row_spec(rw), row_spec(rw),
        ] + [slab(a) for a in cast],
        out_specs=[pl.BlockSpec((None, tb, rw), lambda b, i: (b, i, 0))] + [slab(a) for a in cast],
        out_shape=[jax.ShapeDtypeStruct((bsz, t, rw), BF16)]
        + [jax.ShapeDtypeStruct(a.shape, BF16) for a in cast],
        scratch_shapes=[pltpu.VMEM((8, cols), F32),
                        pltpu.VMEM((rw // QUAD, QUAD, QUAD), F32)],
        compiler_params=pltpu.CompilerParams(
            dimension_semantics=("parallel", "arbitrary"), vmem_limit_bytes=VMEM_LIMIT),
        name="rwkv_mix",
    )(p, mu, w0, wup, a0, aup, gup, k_k, k_a, r_k, lnx_g, lnx_b, *cast)
    return outs[0], outs[1:]


def _sgu_kernel(z_ref, lng_ref, lnb_ref, ws_ref, bs_ref, o_ref):
    tm, width = o_ref.shape
    u = z_ref[:, :width]
    v = z_ref[:, width:]
    mean = jnp.mean(v, axis=-1, keepdims=True)
    vc = v - mean
    var = jnp.mean(vc * vc, axis=-1, keepdims=True)
    vn = vc * lax.rsqrt(var + LN_EPS) * lng_ref[...] + lnb_ref[...]

    tr = lax.broadcasted_iota(jnp.int32, (SGU_CHUNK, SGU_CHUNK), 0)
    tc = lax.broadcasted_iota(jnp.int32, (SGU_CHUNK, SGU_CHUNK), 1)
    causal = tc <= tr
    bias = bs_ref[...]
    for g in range(width // SGU_GROUP):
        w_c = jnp.where(causal, ws_ref[g], 0.0)
        lanes = slice(g * SGU_GROUP, (g + 1) * SGU_GROUP)
        for n in range(tm // SGU_CHUNK):
            rows = slice(n * SGU_CHUNK, (n + 1) * SGU_CHUNK)
            mixed = _mm(w_c, vn[rows, lanes]) + bias[:, lanes]
            o_ref[rows, lanes] = (u[rows, lanes] * mixed).astype(o_ref.dtype)


def _sgu(z, ln_g, ln_b, w_s, b_full, *, tm):
    m, two_w = z.shape
    width = two_w // 2
    return pl.pallas_call(
        _sgu_kernel,
        grid=(m // tm,),
        in_specs=[
            pl.BlockSpec((tm, two_w), lambda i: (i, 0)),
            pl.BlockSpec((1, width), lambda i: (0, 0)),
            pl.BlockSpec((1, width), lambda i: (0, 0)),
            pl.BlockSpec(w_s.shape, lambda i: (0, 0, 0)),
            pl.BlockSpec(b_full.shape, lambda i: (0, 0)),
        ],
        out_specs=pl.BlockSpec((tm, width), lambda i: (i, 0)),
        out_shape=jax.ShapeDtypeStruct((m, width), BF16),
        compiler_params=pltpu.CompilerParams(
            dimension_semantics=("parallel",), vmem_limit_bytes=VMEM_LIMIT),
        name="sgu",
    )(z, ln_g, ln_b, w_s, b_full)


def _merge_kernel(ya_ref, yb_ref, gates_ref, x_ref, wa_ref, wb_ref, wo_ref, g2_ref, h_ref, n_ref):
    d = x_ref.shape[1]
    merged = (gates_ref[:, :d] * _mm(ya_ref[...], wa_ref[...])
              + gates_ref[:, d:] * _mm(yb_ref[...], wb_ref[...]))
    h = x_ref[...] + _mm(merged.astype(BF16), wo_ref[...])
    h_ref[...] = h
    n_ref[...] = _rmsnorm(h, g2_ref[...]).astype(BF16)


def _merge(ya, yb, gates, x, wa, wb, wo, g2, *, tm):
    m, d = x.shape
    rw = ya.shape[1]
    const = lambda a: pl.BlockSpec(a.shape, lambda i: (0, 0), pipeline_mode=pl.Buffered(1))
    return pl.pallas_call(
        _merge_kernel,
        grid=(m // tm,),
        in_specs=[
            pl.BlockSpec((tm, rw), lambda i: (i, 0)),
            pl.BlockSpec((tm, rw), lambda i: (i, 0)),
            pl.BlockSpec((tm, 2 * d), lambda i: (i, 0)),
            pl.BlockSpec((tm, d), lambda i: (i, 0)),
            const(wa), const(wb), const(wo),
            pl.BlockSpec((1, d), lambda i: (0, 0)),
        ],
        out_specs=[pl.BlockSpec((tm, d), lambda i: (i, 0)), pl.BlockSpec((tm, d), lambda i: (i, 0))],
        out_shape=[jax.ShapeDtypeStruct((m, d), F32), jax.ShapeDtypeStruct((m, d), BF16)],
        compiler_params=pltpu.CompilerParams(
            dimension_semantics=("parallel",), vmem_limit_bytes=VMEM_LIMIT),
        name="merge",
    )(ya, yb, gates, x, wa, wb, wo, g2)


def _ffn_kernel(n_ref, h_ref, wg_ref, wu_ref, wd_ref, gf_ref, o_ref, *, sub):
    j = pl.program_id(1)

    @pl.when(j == 0)
    def _():
        o_ref[...] = h_ref[...]

    n = n_ref[...]
    part = None
    for k in range(wg_ref.shape[1] // sub):
        cols = slice(k * sub, (k + 1) * sub)
        gate = _mm(n, wg_ref[:, cols])
        up = _mm(n, wu_ref[:, cols])
        act = (gate * jax.nn.sigmoid(gate) * up).astype(BF16)
        down = _mm(act, wd_ref[cols, :])
        part = down if part is None else part + down
    o_ref[...] += part

    @pl.when(j == pl.num_programs(1) - 1)
    def _():
        o_ref[...] = _rmsnorm(o_ref[...], gf_ref[...])


def _ffn(n2, h1, wg, wu, wd, gf, *, tm, th, sub=256):
    m, d = h1.shape
    hid = wg.shape[1]
    return pl.pallas_call(
        functools.partial(_ffn_kernel, sub=sub),
        grid=(m // tm, hid // th),
        in_specs=[
            pl.BlockSpec((tm, d), lambda i, j: (i, 0)),
            pl.BlockSpec((tm, d), lambda i, j: (i, 0)),
            pl.BlockSpec((d, th), lambda i, j: (0, j)),
            pl.BlockSpec((d, th), lambda i, j: (0, j)),
            pl.BlockSpec((th, d), lambda i, j: (j, 0)),
            pl.BlockSpec((1, d), lambda i, j: (0, 0)),
        ],
        out_specs=pl.BlockSpec((tm, d), lambda i, j: (i, 0)),
        out_shape=jax.ShapeDtypeStruct((m, d), F32),
        compiler_params=pltpu.CompilerParams(
            dimension_semantics=("parallel", "arbitrary"), vmem_limit_bytes=VMEM_LIMIT),
        name="ffn",
    )(n2, h1, wg, wu, wd, gf)


def _pad_cols(a, width):
    return jnp.pad(a, ((0, 0), (0, width - a.shape[1])))


def _block(h, norm_mix_g, w_in, shift_mu, w0, w_lora_up, a0, a_lora_up, g_lora_up, k_k, k_a, r_k,
           lnx_g, lnx_b, w_proj_rwkv, sgu_ln_g, sgu_ln_b, sgu_w, sgu_b, w_proj_sgu, w_out,
           norm_ffn_g, w_ffn_gate, w_ffn_up, w_ffn_down, g_last, *, bsz, seq):
    m, d = h.shape
    rw = w0.shape[0]
    sw = sgu_ln_g.shape[0]
    rcols = shift_mu.shape[0]
    zc = rcols + 2 * sw
    rcols_al = -(-rcols // LANE) * LANE
    row = lambda a: a.reshape(1, -1)

    w_t = w_in.T
    mu = _pad_cols(shift_mu[None, :], rcols_al)

    n1 = _rmsnorm_call(h, row(norm_mix_g), tm=512)
    p = _proj(n1, w_t, 0, rcols_al, tm=2048, tn=512, epilogue="none")
    z = _proj(n1, w_t, rcols, 2 * sw, tm=1024, tn=1024, epilogue="gelu")
    gates = _proj(n1, w_t, zc, 2 * d, tm=1024, tn=1024, epilogue="sigmoid")

    ya, (wa, wb, wo, wg, wu, wd) = _rwkv_mix(
        p.reshape(bsz, seq, -1), mu, row(w0), w_lora_up, row(a0), a_lora_up, g_lora_up,
        row(k_k), row(k_a), row(r_k), row(lnx_g), row(lnx_b),
        (w_proj_rwkv, w_proj_sgu, w_out, w_ffn_gate, w_ffn_up, w_ffn_down), tb=256)

    b_full = jnp.repeat(sgu_b.T, SGU_GROUP, axis=1)
    yb = _sgu(z, row(sgu_ln_g), row(sgu_ln_b), sgu_w, b_full, tm=512)

    h1, n2 = _merge(ya.reshape(m, rw), yb, gates, h, wa, wb, wo, row(norm_ffn_g), tm=256)
    return _ffn(n2, h1, wg, wu, wd, row(g_last), tm=1024, th=512)


def kernel(x, norm_mix_g, w_in, shift_mu, w0, w_lora_up, a0, a_lora_up, g_lora_up, k_k, k_a, r_k,
           lnx_g, lnx_b, w_proj_rwkv, sgu_ln_g, sgu_ln_b, sgu_w, sgu_b, w_proj_sgu, w_out,
           norm_ffn_g, w_ffn_gate, w_ffn_up, w_ffn_down, norm_final_g):
    bsz, seq, d = x.shape
    depth = w_in.shape[0]
    assert depth == 1, "the final rmsnorm is fused into the (single) layer's channel mixer"
    h = x.reshape(bsz * seq, d)
    out = _block(h, norm_mix_g[0], w_in[0], shift_mu[0], w0[0], w_lora_up[0], a0[0], a_lora_up[0],
                 g_lora_up[0], k_k[0], k_a[0], r_k[0], lnx_g[0], lnx_b[0], w_proj_rwkv[0],
                 sgu_ln_g[0], sgu_ln_b[0], sgu_w[0], sgu_b[0], w_proj_sgu[0], w_out[0],
                 norm_ffn_g[0], w_ffn_gate[0], w_ffn_up[0], w_ffn_down[0], norm_final_g,
                 bsz=bsz, seq=seq)
    return out.reshape(bsz, seq, d)
```

```python
import functools

import jax
import jax.numpy as jnp
from jax import lax
from jax.experimental import pallas as pl
from jax.experimental.pallas import tpu as pltpu

F32 = jnp.float32
BF16 = jnp.bfloat16

RMS_EPS = 1e-6
LN_EPS = 1e-5
LNX_EPS = 64e-5

HEAD = 64
WKV_CHUNK = 64
QUAD = 256
HEADS_PER_QUAD = QUAD // HEAD
SGU_CHUNK = 128
SGU_GROUP = 128
LANE = 128
SUBLANE = 8

VMEM_LIMIT = 60 * 1024 * 1024


def _mm(a, b):
    return jnp.dot(a, b, preferred_element_type=F32)


def _mm_nt(a, b):
    return lax.dot_general(a, b, (((1,), (1,)), ((), ())), preferred_element_type=F32)


def _mm_tn(a, b):
    return lax.dot_general(a, b, (((0,), (0,)), ((), ())), preferred_element_type=F32)


def _rmsnorm(x, g):
    return x * lax.rsqrt(jnp.mean(x * x, axis=-1, keepdims=True) + RMS_EPS) * g


def _rmsnorm_kernel(x_ref, g_ref, o_ref):
    o_ref[...] = _rmsnorm(x_ref[...], g_ref[...]).astype(o_ref.dtype)


def _rmsnorm_call(x, g, *, tm):
    m, d = x.shape
    return pl.pallas_call(
        _rmsnorm_kernel,
        grid=(m // tm,),
        in_specs=[pl.BlockSpec((tm, d), lambda i: (i, 0)), pl.BlockSpec((1, d), lambda i: (0, 0))],
        out_specs=pl.BlockSpec((tm, d), lambda i: (i, 0)),
        out_shape=jax.ShapeDtypeStruct((m, d), BF16),
        compiler_params=pltpu.CompilerParams(dimension_semantics=("parallel",)),
        name="rmsnorm",
    )(x, g)


def _proj_kernel(n_ref, wt_ref, o_ref, w16_ref, *, epilogue, sub):
    @pl.when(pl.program_id(1) == 0)
    def _():
        w16_ref[...] = wt_ref[...].astype(BF16)

    for k in range(o_ref.shape[1] // sub):
        cols = slice(k * sub, (k + 1) * sub)
        y = _mm_nt(n_ref[...], w16_ref[cols, :])
        if epilogue == "gelu":
            y = 0.5 * y * (1.0 + lax.erf(y * (2.0 ** -0.5)))
        elif epilogue == "sigmoid":
            y = jax.nn.sigmoid(y)
        o_ref[:, cols] = y.astype(o_ref.dtype)


def _proj(n, w_t, row0, n_cols, *, tm, tn, epilogue, sub=256):
    m, d = n.shape
    return pl.pallas_call(
        functools.partial(_proj_kernel, epilogue=epilogue, sub=sub),
        grid=(n_cols // tn, m // tm),
        in_specs=[
            pl.BlockSpec((tm, d), lambda j, i: (i, 0)),
            pl.BlockSpec((pl.Element(tn), pl.Element(d)),
                         lambda j, i: (pl.multiple_of(row0 + j * tn, SUBLANE), 0)),
        ],
        out_specs=pl.BlockSpec((tm, tn), lambda j, i: (i, j)),
        out_shape=jax.ShapeDtypeStruct((m, n_cols), F32),
        scratch_shapes=[pltpu.VMEM((tn, d), BF16)],
        compiler_params=pltpu.CompilerParams(
            dimension_semantics=("parallel", "arbitrary"), vmem_limit_bytes=VMEM_LIMIT),
        name="proj_" + epilogue,
    )(n, w_t)


_PREP_KEYS = ("ah", "rh", "bh", "kh", "v", "bonus", "g")


def _rwkv_mix_kernel(p_ref, pn_ref, mu_ref, w0_ref, wup_ref, a0_ref, aup_ref, gup_ref, kk_ref,
                     ka_ref, rk_ref, lng_ref, lnb_ref, *rest, rw, win_w, win_a, win_g, n_cast):
    cast_in, o_ref, cast_out = rest[:n_cast], rest[n_cast], rest[n_cast + 1:2 * n_cast + 1]
    state_ref, prep_ref, pc_ref = rest[2 * n_cast + 1:]
    for src_ref, dst_ref in zip(cast_in, cast_out):
        dst_ref[...] = src_ref[...].astype(dst_ref.dtype)

    tb = p_ref.shape[0]
    c_sz = WKV_CHUNK
    n_quads = rw // QUAD
    qs = range(n_quads)
    lanes = [slice(q * QUAD, (q + 1) * QUAD) for q in qs]
    rows = HEADS_PER_QUAD * c_sz
    assert rows == QUAD

    @pl.when(pl.program_id(1) == 0)
    def _():
        state_ref[...] = jnp.zeros_like(state_ref)

    ri = lax.broadcasted_iota(jnp.int32, (rows, QUAD), 0)
    ci = lax.broadcasted_iota(jnp.int32, (rows, QUAD), 1)
    ones_bd = jnp.where((ri // HEAD) == (ci // HEAD), 1.0, 0.0).astype(F32)
    avg_bd = ones_bd * (1.0 / HEAD)
    head_mask = (ri // c_sz) == (ci // HEAD)
    same = (ri // c_sz) == (ci // c_sz)
    strict = same & (ci < ri)
    incl = same & (ci <= ri)
    leaf = (ri // 16) == (ci // 16)
    in32 = (ri // 32) == (ci // 32)
    eye = ri == ci
    eye_f = jnp.where(eye, 1.0, 0.0).astype(F32)
    tr = lax.broadcasted_iota(jnp.int32, (c_sz, c_sz), 0)
    tc = lax.broadcasted_iota(jnp.int32, (c_sz, c_sz), 1)
    tri = jnp.where(tc <= tr, 1.0, 0.0).astype(BF16)
    crow = lax.broadcasted_iota(jnp.int32, (c_sz, 1), 0)

    def head_sums(xs):
        lhs = jnp.concatenate([x[:, lanes[q]] for x in xs for q in qs], axis=0)
        out = _mm(lhs, ones_bd)
        return [jnp.concatenate([out[(i * n_quads + q) * c_sz:(i * n_quads + q + 1) * c_sz]
                                 for q in qs], axis=1) for i in range(len(xs))]

    def stack(x):
        return jnp.where(head_mask, jnp.concatenate([x] * HEADS_PER_QUAD, axis=0), 0.0)

    def expand(x):
        return jnp.concatenate([x, x], axis=1)

    def prep_chunk(p, before):
        prev = jnp.where(crow == 0, before, pltpu.roll(p, 1, 0))
        sh = p + (prev - p) * mu_ref[...]
        r = sh[:, 0:rw]
        k = sh[:, rw:2 * rw]
        v = sh[:, 2 * rw:3 * rw]
        xw = sh[:, win_w[0]:win_w[1]]
        xa = sh[:, win_a[0]:win_a[1]]
        xg = sh[:, win_g[0]:win_g[1]]

        wl = w0_ref[...] + _mm(jnp.tanh(xw), wup_ref[...])
        lw = (-(2.718281828459045 ** -0.5)) * jax.nn.sigmoid(wl)
        a = jax.nn.sigmoid(a0_ref[...] + _mm(xa, aup_ref[...]))
        g = _mm(jax.nn.sigmoid(xg), gup_ref[...])

        kk = k * kk_ref[...]
        k2 = k * (1.0 + (a - 1.0) * ka_ref[...])
        kk_ss, rk_sum = head_sums([kk * kk, r * k2 * rk_ref[...]])
        kk = kk * jnp.minimum(lax.rsqrt(kk_ss), 1e12)
        bonus = rk_sum * v

        h1 = lw.astype(BF16)
        r1 = lw - h1.astype(F32)
        h2 = r1.astype(BF16)
        h3 = (r1 - h2.astype(F32)).astype(BF16)
        cum = _mm(tri, h1) + _mm(tri, h2) + _mm(tri, h3)
        ec = jnp.exp(cum)
        eci = jnp.exp(-cum)
        ec_prev = jnp.where(crow == 0, 1.0, pltpu.roll(ec, 1, 0))
        return dict(ah=-kk * ec_prev, rh=r * ec, bh=kk * a * eci, kh=k2 * eci, v=v,
                    pc=ec[c_sz - 1:c_sz, :], bonus=bonus, g=g)

    def wkv_chunk(d):
        ah = [d["ah"][:, lanes[q]] for q in qs]
        rh = [d["rh"][:, lanes[q]] for q in qs]
        bh = [d["bh"][:, lanes[q]] for q in qs]
        kh = [d["kh"][:, lanes[q]] for q in qs]
        vv = [d["v"][:, lanes[q]] for q in qs]
        pc = [d["pc"][:, lanes[q]] for q in qs]

        ah_s = [stack(x) for x in ah]
        rh_s = [stack(x) for x in rh]
        v_s = [stack(x) for x in vv]
        s = [_mm_nt(jnp.concatenate([ah_s[q], rh_s[q]], axis=0),
                    jnp.concatenate([bh[q], bh[q], kh[q], kh[q]], axis=0)) for q in qs]
        a_ab = [jnp.where(strict, expand(s[q][:rows, :QUAD // 2]), 0.0) for q in qs]
        a_ak = [jnp.where(strict, expand(s[q][:rows, QUAD // 2:]), 0.0) for q in qs]
        a_rb = [jnp.where(incl, expand(s[q][rows:, :QUAD // 2]), 0.0) for q in qs]
        a_rk = [jnp.where(incl, expand(s[q][rows:, QUAD // 2:]), 0.0) for q in qs]

        a_d = [jnp.where(leaf, a_ab[q], 0.0) for q in qs]
        t_inv = [eye_f + a_d[q] for q in qs]
        pw = [_mm(a_d[q], a_d[q]) for q in qs]
        t_inv = [t_inv[q] + _mm(pw[q], t_inv[q]) for q in qs]
        pw = [_mm(pw[q], pw[q]) for q in qs]
        t_inv = [t_inv[q] + _mm(pw[q], t_inv[q]) for q in qs]
        pw = [_mm(pw[q], pw[q]) for q in qs]
        t_inv = [t_inv[q] + _mm(pw[q], t_inv[q]) for q in qs]
        a_32 = [jnp.where(in32 & jnp.logical_not(leaf), a_ab[q], 0.0) for q in qs]
        ta = [_mm(t_inv[q], a_32[q]) for q in qs]
        t_inv = [t_inv[q] + _mm(ta[q], t_inv[q]) for q in qs]
        a_64 = [jnp.where(in32, 0.0, a_ab[q]) for q in qs]
        ta = [_mm(t_inv[q], a_64[q]) for q in qs]
        t_inv = [t_inv[q] + _mm(ta[q], t_inv[q]) for q in qs]

        akv = [_mm(a_ak[q], v_s[q]) for q in qs]
        wu = [_mm(t_inv[q], jnp.concatenate([ah_s[q], akv[q]], axis=1)) for q in qs]
        bk = [jnp.concatenate([stack(bh[q] * pc[q]), stack(kh[q] * pc[q])], axis=0) for q in qs]
        pc_col = [jnp.sum(jnp.where(eye, pc[q], 0.0), axis=1, keepdims=True) for q in qs]
        return dict(rh_s=rh_s, v_s=v_s, a_rb=a_rb, a_rk=a_rk, wu=wu, bk=bk, pc_col=pc_col)

    def wkv_state(w):
        rh_s, v_s, a_rb, a_rk = w["rh_s"], w["v_s"], w["a_rb"], w["a_rk"]
        wu, bk, pc_col = w["wu"], w["bk"], w["pc_col"]
        m_st = [state_ref[q] for q in qs]
        u_s = [_mm(wu[q][:, :QUAD], m_st[q]) + wu[q][:, QUAD:] for q in qs]
        muv = [jnp.concatenate([m_st[q], u_s[q], v_s[q]], axis=0) for q in qs]
        y_s = [_mm(jnp.concatenate([rh_s[q], a_rb[q], a_rk[q]], axis=1), muv[q]) for q in qs]
        ys = []
        for q in qs:
            y = y_s[q][0:c_sz]
            for h in range(1, HEADS_PER_QUAD):
                y = y + y_s[q][h * c_sz:(h + 1) * c_sz]
            ys.append(y)
            state_ref[q] = (pc_col[q] * m_st[q]
                            + _mm_tn(bk[q], jnp.concatenate([u_s[q], v_s[q]], axis=0)))
        return ys

    def post_chunk(c, ys, d):
        y = jnp.concatenate(ys, axis=0)
        yc = y - _mm(y, avg_bd)
        var = _mm(yc * yc, avg_bd)
        yn = yc * lax.rsqrt(var + LNX_EPS)
        yn = jnp.concatenate([yn[q * c_sz:(q + 1) * c_sz] for q in qs], axis=1)
        out = (yn * lng_ref[...] + lnb_ref[...] + d["bonus"]) * d["g"]
        o_ref[c * c_sz:(c + 1) * c_sz, :] = out.astype(o_ref.dtype)

    def store_prep(d):
        for n, key in enumerate(_PREP_KEYS):
            prep_ref[n] = d[key]
        pc_ref[0:1, :] = d["pc"]

    def load_prep():
        d = {key: prep_ref[n] for n, key in enumerate(_PREP_KEYS)}
        d["pc"] = pc_ref[0:1, :]
        return d

    @pl.when((pl.program_id(0) == 0) & (pl.program_id(1) == 0))
    def _():
        store_prep(prep_chunk(p_ref[0:c_sz, :], jnp.zeros((1, p_ref.shape[1]), F32)))

    n_chunks = tb // c_sz
    cur = load_prep()
    cur_w = wkv_chunk(cur)
    for c in range(n_chunks):
        r1 = (c + 1) * c_sz
        if c + 1 < n_chunks:
            nxt = prep_chunk(p_ref[r1:r1 + c_sz, :], p_ref[r1 - 1:r1, :])
            nxt_w = wkv_chunk(nxt)
        else:
            last_block = pl.program_id(1) == pl.num_programs(1) - 1
            before = jnp.where(last_block, 0.0, p_ref[tb - 1:tb, :])
            store_prep(prep_chunk(pn_ref[...], before))
        post_chunk(c, wkv_state(cur_w), cur)
        cur, cur_w = nxt, nxt_w


def _lane_window(start, stop):
    return (start // LANE) * LANE, -(-stop // LANE) * LANE


def _pad_rows_to_window(w, start, window):
    return jnp.pad(w, ((start - window[0], window[1] - start - w.shape[0]), (0, 0)))


def _rwkv_mix(p, mu, w0, w_lora_up, a0, a_lora_up, g_lora_up, k_k, k_a, r_k, lnx_g, lnx_b, cast, *,
              tb):
    bsz, t, cols = p.shape
    rw = w0.shape[1]
    o1 = 3 * rw
    o2 = o1 + w_lora_up.shape[0]
    o3 = o2 + a_lora_up.shape[0]
    o4 = o3 + g_lora_up.shape[0]
    win_w, win_a, win_g = _lane_window(o1, o2), _lane_window(o2, o3), _lane_window(o3, o4)
    assert win_g[1] == cols
    wup = _pad_rows_to_window(w_lora_up, o1, win_w)
    aup = _pad_rows_to_window(a_lora_up, o2, win_a)
    gup = _pad_rows_to_window(g_lora_up, o3, win_g)
    row_spec = lambda w: pl.BlockSpec((1, w), lambda b, i: (0, 0))
    full = lambda a: pl.BlockSpec(a.shape, lambda b, i: (0, 0))
    nt = t // tb
    steps = bsz * nt
    slab = lambda a: pl.BlockSpec((a.shape[0] // steps, a.shape[1]), lambda b, i: (b * nt + i, 0))
    for a in cast:
        assert a.shape[0] % (steps * 2 * SUBLANE) == 0, a.shape
    chunks_per_block = tb // WKV_CHUNK

    def next_first_chunk(b, i):
        flat = jnp.minimum(b * nt + i + 1, steps - 1)
        return lax.div(flat, nt), lax.rem(flat, nt) * chunks_per_block, 0

    outs = pl.pallas_call(
        functools.partial(_rwkv_mix_kernel, rw=rw, win_w=win_w, win_a=win_a, win_g=win_g,
                          n_cast=len(cast)),
        grid=(bsz, nt),
        in_specs=[
            pl.BlockSpec((None, tb, cols), lambda b, i: (b, i, 0)),
            pl.BlockSpec((None, WKV_CHUNK, cols), next_first_chunk),
            row_spec(cols), row_spec(rw), full(wup), row_spec(rw), full(aup), full(gup),
            row_spec(rw), row_spec(rw), row_spec(rw), row_spec(rw), row_spec(rw),
        ] + [slab(a) for a in cast],
        out_specs=[pl.BlockSpec((None, tb, rw), lambda b, i: (b, i, 0))] + [slab(a) for a in cast],
        out_shape=[jax.ShapeDtypeStruct((bsz, t, rw), BF16)]
        + [jax.ShapeDtypeStruct(a.shape, BF16) for a in cast],
        scratch_shapes=[pltpu.VMEM((rw // QUAD, QUAD, QUAD), F32),
                        pltpu.VMEM((len(_PREP_KEYS), WKV_CHUNK, rw), F32),
                        pltpu.VMEM((SUBLANE, rw), F32)],
        compiler_params=pltpu.CompilerParams(
            dimension_semantics=("arbitrary", "arbitrary"), vmem_limit_bytes=VMEM_LIMIT),
        name="rwkv_mix",
    )(p, p, mu, w0, wup, a0, aup, gup, k_k, k_a, r_k, lnx_g, lnx_b, *cast)
    return outs[0], outs[1:]


def _sgu_kernel(z_ref, lng_ref, lnb_ref, ws_ref, bs_ref, o_ref):
    tm, width = o_ref.shape
    u = z_ref[:, :width]
    v = z_ref[:, width:]
    mean = jnp.mean(v, axis=-1, keepdims=True)
    vc = v - mean
    var = jnp.mean(vc * vc, axis=-1, keepdims=True)
    vn = vc * lax.rsqrt(var + LN_EPS) * lng_ref[...] + lnb_ref[...]

    tr = lax.broadcasted_iota(jnp.int32, (SGU_CHUNK, SGU_CHUNK), 0)
    tc = lax.broadcasted_iota(jnp.int32, (SGU_CHUNK, SGU_CHUNK), 1)
    causal = tc <= tr
    bias = bs_ref[...]
    for g in range(width // SGU_GROUP):
        w_c = jnp.where(causal, ws_ref[g], 0.0)
        lanes = slice(g * SGU_GROUP, (g + 1) * SGU_GROUP)
        for n in range(tm // SGU_CHUNK):
            rows = slice(n * SGU_CHUNK, (n + 1) * SGU_CHUNK)
            mixed = _mm(w_c, vn[rows, lanes]) + bias[:, lanes]
            o_ref[rows, lanes] = (u[rows, lanes] * mixed).astype(o_ref.dtype)


def _sgu(z, ln_g, ln_b, w_s, b_full, *, tm):
    m, two_w = z.shape
    width = two_w // 2
    return pl.pallas_call(
        _sgu_kernel,
        grid=(m // tm,),
        in_specs=[
            pl.BlockSpec((tm, two_w), lambda i: (i, 0)),
            pl.BlockSpec((1, width), lambda i: (0, 0)),
            pl.BlockSpec((1, width), lambda i: (0, 0)),
            pl.BlockSpec(w_s.shape, lambda i: (0, 0, 0)),
            pl.BlockSpec(b_full.shape, lambda i: (0, 0)),
        ],
        out_specs=pl.BlockSpec((tm, width), lambda i: (i, 0)),
        out_shape=jax.ShapeDtypeStruct((m, width), BF16),
        compiler_params=pltpu.CompilerParams(
            dimension_semantics=("parallel",), vmem_limit_bytes=VMEM_LIMIT),
        name="sgu",
    )(z, ln_g, ln_b, w_s, b_full)


def _merge_kernel(ya_ref, yb_ref, gates_ref, x_ref, wa_ref, wb_ref, wo_ref, g2_ref, h_ref, n_ref):
    d = x_ref.shape[1]
    merged = (gates_ref[:, :d] * _mm(ya_ref[...], wa_ref[...])
              + gates_ref[:, d:] * _mm(yb_ref[...], wb_ref[...]))
    h = x_ref[...] + _mm(merged.astype(BF16), wo_ref[...])
    h_ref[...] = h
    n_ref[...] = _rmsnorm(h, g2_ref[...]).astype(BF16)


def _merge(ya, yb, gates, x, wa, wb, wo, g2, *, tm):
    m, d = x.shape
    rw = ya.shape[1]
    const = lambda a: pl.BlockSpec(a.shape, lambda i: (0, 0), pipeline_mode=pl.Buffered(1))
    return pl.pallas_call(
        _merge_kernel,
        grid=(m // tm,),
        in_specs=[
            pl.BlockSpec((tm, rw), lambda i: (i, 0)),
            pl.BlockSpec((tm, rw), lambda i: (i, 0)),
            pl.BlockSpec((tm, 2 * d), lambda i: (i, 0)),
            pl.BlockSpec((tm, d), lambda i: (i, 0)),
            const(wa), const(wb), const(wo),
            pl.BlockSpec((1, d), lambda i: (0, 0)),
        ],
        out_specs=[pl.BlockSpec((tm, d), lambda i: (i, 0)), pl.BlockSpec((tm, d), lambda i: (i, 0))],
        out_shape=[jax.ShapeDtypeStruct((m, d), F32), jax.ShapeDtypeStruct((m, d), BF16)],
        compiler_params=pltpu.CompilerParams(
            dimension_semantics=("parallel",), vmem_limit_bytes=VMEM_LIMIT),
        name="merge",
    )(ya, yb, gates, x, wa, wb, wo, g2)


def _ffn_kernel(n_ref, h_ref, wg_ref, wu_ref, wd_ref, gf_ref, o_ref, *, sub):
    j = pl.program_id(1)

    @pl.when(j == 0)
    def _():
        o_ref[...] = h_ref[...]

    n = n_ref[...]
    part = None
    for k in range(wg_ref.shape[1] // sub):
        cols = slice(k * sub, (k + 1) * sub)
        gate = _mm(n, wg_ref[:, cols])
        up = _mm(n, wu_ref[:, cols])
        act = (gate * jax.nn.sigmoid(gate) * up).astype(BF16)
        down = _mm(act, wd_ref[cols, :])
        part = down if part is None else part + down
    o_ref[...] += part

    @pl.when(j == pl.num_programs(1) - 1)
    def _():
        o_ref[...] = _rmsnorm(o_ref[...], gf_ref[...])


def _ffn(n2, h1, wg, wu, wd, gf, *, tm, th, sub=256):
    m, d = h1.shape
    hid = wg.shape[1]
    return pl.pallas_call(
        functools.partial(_ffn_kernel, sub=sub),
        grid=(m // tm, hid // th),
        in_specs=[
            pl.BlockSpec((tm, d), lambda i, j: (i, 0)),
            pl.BlockSpec((tm, d), lambda i, j: (i, 0)),
            pl.BlockSpec((d, th), lambda i, j: (0, j)),
            pl.BlockSpec((d, th), lambda i, j: (0, j)),
            pl.BlockSpec((th, d), lambda i, j: (j, 0)),
            pl.BlockSpec((1, d), lambda i, j: (0, 0)),
        ],
        out_specs=pl.BlockSpec((tm, d), lambda i, j: (i, 0)),
        out_shape=jax.ShapeDtypeStruct((m, d), F32),
        compiler_params=pltpu.CompilerParams(
            dimension_semantics=("parallel", "arbitrary"), vmem_limit_bytes=VMEM_LIMIT),
        name="ffn",
    )(n2, h1, wg, wu, wd, gf)


def _pad_cols(a, width):
    return jnp.pad(a, ((0, 0), (0, width - a.shape[1])))


def _block(h, norm_mix_g, w_in, shift_mu, w0, w_lora_up, a0, a_lora_up, g_lora_up, k_k, k_a, r_k,
           lnx_g, lnx_b, w_proj_rwkv, sgu_ln_g, sgu_ln_b, sgu_w, sgu_b, w_proj_sgu, w_out,
           norm_ffn_g, w_ffn_gate, w_ffn_up, w_ffn_down, g_last, *, bsz, seq):
    m, d = h.shape
    rw = w0.shape[0]
    sw = sgu_ln_g.shape[0]
    rcols = shift_mu.shape[0]
    zc = rcols + 2 * sw
    rcols_al = -(-rcols // LANE) * LANE
    row = lambda a: a.reshape(1, -1)

    w_t = w_in.T
    mu = _pad_cols(shift_mu[None, :], rcols_al)

    n1 = _rmsnorm_call(h, row(norm_mix_g), tm=512)
    p = _proj(n1, w_t, 0, rcols_al, tm=2048, tn=512, epilogue="none")
    z = _proj(n1, w_t, rcols, 2 * sw, tm=2048, tn=1024, epilogue="gelu")
    gates = _proj(n1, w_t, zc, 2 * d, tm=2048, tn=1024, epilogue="sigmoid")

    ya, (wa, wb, wo, wg, wu, wd) = _rwkv_mix(
        p.reshape(bsz, seq, -1), mu, row(w0), w_lora_up, row(a0), a_lora_up, g_lora_up,
        row(k_k), row(k_a), row(r_k), row(lnx_g), row(lnx_b),
        (w_proj_rwkv, w_proj_sgu, w_out, w_ffn_gate, w_ffn_up, w_ffn_down), tb=256)

    b_full = jnp.repeat(sgu_b.T, SGU_GROUP, axis=1)
    yb = _sgu(z, row(sgu_ln_g), row(sgu_ln_b), sgu_w, b_full, tm=512)

    h1, n2 = _merge(ya.reshape(m, rw), yb, gates, h, wa, wb, wo, row(norm_ffn_g), tm=256)
    return _ffn(n2, h1, wg, wu, wd, row(g_last), tm=1024, th=512)


def kernel(x, norm_mix_g, w_in, shift_mu, w0, w_lora_up, a0, a_lora_up, g_lora_up, k_k, k_a, r_k,
           lnx_g, lnx_b, w_proj_rwkv, sgu_ln_g, sgu_ln_b, sgu_w, sgu_b, w_proj_sgu, w_out,
           norm_ffn_g, w_ffn_gate, w_ffn_up, w_ffn_down, norm_final_g):
    bsz, seq, d = x.shape
    depth = w_in.shape[0]
    assert depth == 1, "the final rmsnorm is fused into the (single) layer's channel mixer"
    h = x.reshape(bsz * seq, d)
    out = _block(h, norm_mix_g[0], w_in[0], shift_mu[0], w0[0], w_lora_up[0], a0[0], a_lora_up[0],
                 g_lora_up[0], k_k[0], k_a[0], r_k[0], lnx_g[0], lnx_b[0], w_proj_rwkv[0],
                 sgu_ln_g[0], sgu_ln_b[0], sgu_w[0], sgu_b[0], w_proj_sgu[0], w_out[0],
                 norm_ffn_g[0], w_ffn_gate[0], w_ffn_up[0], w_ffn_down[0], norm_final_g,
                 bsz=bsz, seq=seq)
    return out.reshape(bsz, seq, d)
```

```python
import functools

import jax
import jax.numpy as jnp
from jax import lax
from jax.experimental import pallas as pl
from jax.experimental.pallas import tpu as pltpu

F32 = jnp.float32
BF16 = jnp.bfloat16

RMS_EPS = 1e-6
LN_EPS = 1e-5
LNX_EPS = 64e-5

HEAD = 64
WKV_CHUNK = 64
QUAD = 256
HEADS_PER_QUAD = QUAD // HEAD
SGU_CHUNK = 128
SGU_GROUP = 128
LANE = 128
SUBLANE = 8

VMEM_LIMIT = 60 * 1024 * 1024


def _mm(a, b):
    return jnp.dot(a, b, preferred_element_type=F32)


def _mm_nt(a, b):
    return lax.dot_general(a, b, (((1,), (1,)), ((), ())), preferred_element_type=F32)


def _mm_tn(a, b):
    return lax.dot_general(a, b, (((0,), (0,)), ((), ())), preferred_element_type=F32)


def _rmsnorm(x, g):
    return x * lax.rsqrt(jnp.mean(x * x, axis=-1, keepdims=True) + RMS_EPS) * g


def _rmsnorm_kernel(x_ref, g_ref, o_ref):
    o_ref[...] = _rmsnorm(x_ref[...], g_ref[...]).astype(o_ref.dtype)


def _rmsnorm_call(x, g, *, tm):
    m, d = x.shape
    return pl.pallas_call(
        _rmsnorm_kernel,
        grid=(m // tm,),
        in_specs=[pl.BlockSpec((tm, d), lambda i: (i, 0)), pl.BlockSpec((1, d), lambda i: (0, 0))],
        out_specs=pl.BlockSpec((tm, d), lambda i: (i, 0)),
        out_shape=jax.ShapeDtypeStruct((m, d), BF16),
        compiler_params=pltpu.CompilerParams(dimension_semantics=("parallel",)),
        name="rmsnorm",
    )(x, g)


def _proj_kernel(n_ref, wt_ref, o_ref, w16_ref, *, epilogue, sub):
    @pl.when(pl.program_id(1) == 0)
    def _():
        w16_ref[...] = wt_ref[...].astype(BF16)

    for k in range(o_ref.shape[1] // sub):
        cols = slice(k * sub, (k + 1) * sub)
        y = _mm_nt(n_ref[...], w16_ref[cols, :])
        if epilogue == "gelu":
            y = 0.5 * y * (1.0 + lax.erf(y * (2.0 ** -0.5)))
        elif epilogue == "sigmoid":
            y = jax.nn.sigmoid(y)
        o_ref[:, cols] = y.astype(o_ref.dtype)


def _proj(n, w_t, row0, n_cols, *, tm, tn, epilogue, sub=256, w_buffers=2):
    m, d = n.shape
    return pl.pallas_call(
        functools.partial(_proj_kernel, epilogue=epilogue, sub=sub),
        grid=(n_cols // tn, m // tm),
        in_specs=[
            pl.BlockSpec((tm, d), lambda j, i: (i, 0)),
            pl.BlockSpec((pl.Element(tn), pl.Element(d)),
                         lambda j, i: (pl.multiple_of(row0 + j * tn, SUBLANE), 0),
                         pipeline_mode=pl.Buffered(w_buffers)),
        ],
        out_specs=pl.BlockSpec((tm, tn), lambda j, i: (i, j)),
        out_shape=jax.ShapeDtypeStruct((m, n_cols), F32),
        scratch_shapes=[pltpu.VMEM((tn, d), BF16)],
        compiler_params=pltpu.CompilerParams(
            dimension_semantics=("parallel", "arbitrary"), vmem_limit_bytes=VMEM_LIMIT),
        name="proj_" + epilogue,
    )(n, w_t)


def _rwkv_mix_kernel(p_ref, mu_ref, w0_ref, wup_ref, a0_ref, aup_ref, gup_ref, kk_ref, ka_ref,
                     rk_ref, lng_ref, lnb_ref, *rest, rw, win_w, win_a, win_g, n_cast):
    cast_in, o_ref, cast_out = rest[:n_cast], rest[n_cast], rest[n_cast + 1:2 * n_cast + 1]
    carry_ref, state_ref = rest[2 * n_cast + 1:]
    for src_ref, dst_ref in zip(cast_in, cast_out):
        dst_ref[...] = src_ref[...].astype(dst_ref.dtype)

    tb = p_ref.shape[0]
    c_sz = WKV_CHUNK
    n_quads = rw // QUAD
    qs = range(n_quads)
    lanes = [slice(q * QUAD, (q + 1) * QUAD) for q in qs]
    rows = HEADS_PER_QUAD * c_sz
    assert rows == QUAD

    @pl.when(pl.program_id(1) == 0)
    def _():
        carry_ref[...] = jnp.zeros_like(carry_ref)
        state_ref[...] = jnp.zeros_like(state_ref)

    ri = lax.broadcasted_iota(jnp.int32, (rows, QUAD), 0)
    ci = lax.broadcasted_iota(jnp.int32, (rows, QUAD), 1)
    ones_bd = jnp.where((ri // HEAD) == (ci // HEAD), 1.0, 0.0).astype(F32)
    avg_bd = ones_bd * (1.0 / HEAD)
    head_mask = (ri // c_sz) == (ci // HEAD)
    same = (ri // c_sz) == (ci // c_sz)
    strict = same & (ci < ri)
    incl = same & (ci <= ri)
    leaf = (ri // 16) == (ci // 16)
    in32 = (ri // 32) == (ci // 32)
    eye = ri == ci
    eye_f = jnp.where(eye, 1.0, 0.0).astype(F32)
    tr = lax.broadcasted_iota(jnp.int32, (c_sz, c_sz), 0)
    tc = lax.broadcasted_iota(jnp.int32, (c_sz, c_sz), 1)
    tri = jnp.where(tc <= tr, 1.0, 0.0).astype(BF16)
    crow = lax.broadcasted_iota(jnp.int32, (c_sz, 1), 0)

    def head_sums(xs):
        lhs = jnp.concatenate([x[:, lanes[q]] for x in xs for q in qs], axis=0)
        out = _mm(lhs, ones_bd)
        return [jnp.concatenate([out[(i * n_quads + q) * c_sz:(i * n_quads + q + 1) * c_sz]
                                 for q in qs], axis=1) for i in range(len(xs))]

    def b16(x):
        return x.astype(BF16)

    def stack(x):
        return jnp.where(head_mask, jnp.concatenate([b16(x)] * HEADS_PER_QUAD, axis=0), 0.0)

    def expand(x):
        return jnp.concatenate([x, x], axis=1)

    def prep_chunk(c):
        r0 = c * c_sz
        p = p_ref[r0:r0 + c_sz, :]
        before = carry_ref[0:1, :] if c == 0 else p_ref[r0 - 1:r0, :]
        prev = jnp.where(crow == 0, before, pltpu.roll(p, 1, 0))
        sh = p + (prev - p) * mu_ref[...]
        r = sh[:, 0:rw]
        k = sh[:, rw:2 * rw]
        v = sh[:, 2 * rw:3 * rw]
        xw = sh[:, win_w[0]:win_w[1]]
        xa = sh[:, win_a[0]:win_a[1]]
        xg = sh[:, win_g[0]:win_g[1]]

        wl = w0_ref[...] + _mm(jnp.tanh(xw), wup_ref[...])
        lw = (-(2.718281828459045 ** -0.5)) * jax.nn.sigmoid(wl)
        a = jax.nn.sigmoid(a0_ref[...] + _mm(xa, aup_ref[...]))
        g = _mm(jax.nn.sigmoid(xg), gup_ref[...])

        kk = k * kk_ref[...]
        k2 = k * (1.0 + (a - 1.0) * ka_ref[...])
        kk_ss, rk_sum = head_sums([kk * kk, r * k2 * rk_ref[...]])
        kk = kk * jnp.minimum(lax.rsqrt(kk_ss), 1e12)
        bonus = rk_sum * v

        h1 = lw.astype(BF16)
        r1 = lw - h1.astype(F32)
        h2 = r1.astype(BF16)
        h3 = (r1 - h2.astype(F32)).astype(BF16)
        cum = _mm(tri, h1) + _mm(tri, h2) + _mm(tri, h3)
        ec = jnp.exp(cum)
        eci = jnp.exp(-cum)
        ec_prev = jnp.where(crow == 0, 1.0, pltpu.roll(ec, 1, 0))
        return dict(ah=-kk * ec_prev, rh=r * ec, bh=kk * a * eci, kh=k2 * eci, v=v,
                    pc=ec[c_sz - 1:c_sz, :], bonus=bonus, g=g)

    def wkv_chunk(d):
        ah = [d["ah"][:, lanes[q]] for q in qs]
        rh = [d["rh"][:, lanes[q]] for q in qs]
        bh = [d["bh"][:, lanes[q]] for q in qs]
        kh = [d["kh"][:, lanes[q]] for q in qs]
        vv = [d["v"][:, lanes[q]] for q in qs]
        pc = [d["pc"][:, lanes[q]] for q in qs]

        ah_s = [stack(x) for x in ah]
        rh_s = [stack(x) for x in rh]
        v_s = [stack(x) for x in vv]
        s = [_mm_nt(jnp.concatenate([ah_s[q], rh_s[q]], axis=0),
                    b16(jnp.concatenate([bh[q], bh[q], kh[q], kh[q]], axis=0))) for q in qs]
        a_ab = [jnp.where(strict, expand(s[q][:rows, :QUAD // 2]), 0.0) for q in qs]
        a_ak = [b16(jnp.where(strict, expand(s[q][:rows, QUAD // 2:]), 0.0)) for q in qs]
        a_rb = [b16(jnp.where(incl, expand(s[q][rows:, :QUAD // 2]), 0.0)) for q in qs]
        a_rk = [b16(jnp.where(incl, expand(s[q][rows:, QUAD // 2:]), 0.0)) for q in qs]

        a_d = [jnp.where(leaf, a_ab[q], 0.0) for q in qs]
        t_inv = [eye_f + a_d[q] for q in qs]
        pw = [b16(a_d[q]) for q in qs]
        for _ in range(3):
            pw = [b16(_mm(pw[q], pw[q])) for q in qs]
            t_inv = [t_inv[q] + _mm(pw[q], b16(t_inv[q])) for q in qs]
        for a_off in ([b16(jnp.where(in32 & jnp.logical_not(leaf), a_ab[q], 0.0)) for q in qs],
                      [b16(jnp.where(in32, 0.0, a_ab[q])) for q in qs]):
            t16 = [b16(t_inv[q]) for q in qs]
            ta = [b16(_mm(t16[q], a_off[q])) for q in qs]
            t_inv = [t_inv[q] + _mm(ta[q], t16[q]) for q in qs]

        akv = [b16(_mm(a_ak[q], v_s[q])) for q in qs]
        wu = [_mm(b16(t_inv[q]), jnp.concatenate([ah_s[q], akv[q]], axis=1)) for q in qs]
        wt = [b16(wu[q][:, :QUAD]) for q in qs]
        ut = [wu[q][:, QUAD:] for q in qs]
        bk = [jnp.concatenate([stack(bh[q] * pc[q]), stack(kh[q] * pc[q])], axis=0) for q in qs]
        pc_col = [jnp.sum(jnp.where(eye, pc[q], 0.0), axis=1, keepdims=True) for q in qs]
        return dict(rh_s=rh_s, v_s=v_s, a_rb=a_rb, a_rk=a_rk, wt=wt, ut=ut, bk=bk, pc_col=pc_col)

    def wkv_state(w):
        rh_s, v_s, a_rb, a_rk = w["rh_s"], w["v_s"], w["a_rb"], w["a_rk"]
        wt, ut, bk, pc_col = w["wt"], w["ut"], w["bk"], w["pc_col"]
        m_st = [state_ref[q] for q in qs]
        m16 = [b16(m_st[q]) for q in qs]
        u16 = [b16(_mm(wt[q], m16[q]) + ut[q]) for q in qs]
        muv = [jnp.concatenate([m16[q], u16[q], v_s[q]], axis=0) for q in qs]
        y_s = [_mm(jnp.concatenate([rh_s[q], a_rb[q], a_rk[q]], axis=1), muv[q]) for q in qs]
        ys = []
        for q in qs:
            y = y_s[q][0:c_sz]
            for h in range(1, HEADS_PER_QUAD):
                y = y + y_s[q][h * c_sz:(h + 1) * c_sz]
            ys.append(y)
            state_ref[q] = (pc_col[q] * m_st[q]
                            + _mm_tn(bk[q], jnp.concatenate([u16[q], v_s[q]], axis=0)))
        return ys

    def post_chunk(c, ys, d):
        y = jnp.concatenate(ys, axis=0)
        yc = y - _mm(y, avg_bd)
        var = _mm(yc * yc, avg_bd)
        yn = yc * lax.rsqrt(var + LNX_EPS)
        yn = jnp.concatenate([yn[q * c_sz:(q + 1) * c_sz] for q in qs], axis=1)
        out = (yn * lng_ref[...] + lnb_ref[...] + d["bonus"]) * d["g"]
        o_ref[c * c_sz:(c + 1) * c_sz, :] = out.astype(o_ref.dtype)

    n_chunks = tb // c_sz
    nxt = prep_chunk(0)
    nxt_w = wkv_chunk(nxt)
    for c in range(n_chunks):
        cur, cur_w = nxt, nxt_w
        if c + 1 < n_chunks:
            nxt = prep_chunk(c + 1)
            nxt_w = wkv_chunk(nxt)
        post_chunk(c, wkv_state(cur_w), cur)
    carry_ref[0:1, :] = p_ref[tb - 1:tb, :]


def _lane_window(start, stop):
    return (start // LANE) * LANE, -(-stop // LANE) * LANE


def _pad_rows_to_window(w, start, window):
    return jnp.pad(w, ((start - window[0], window[1] - start - w.shape[0]), (0, 0)))


def _rwkv_mix(p, mu, w0, w_lora_up, a0, a_lora_up, g_lora_up, k_k, k_a, r_k, lnx_g, lnx_b, cast, *,
              tb):
    bsz, t, cols = p.shape
    rw = w0.shape[1]
    o1 = 3 * rw
    o2 = o1 + w_lora_up.shape[0]
    o3 = o2 + a_lora_up.shape[0]
    o4 = o3 + g_lora_up.shape[0]
    win_w, win_a, win_g = _lane_window(o1, o2), _lane_window(o2, o3), _lane_window(o3, o4)
    assert win_g[1] == cols
    wup = _pad_rows_to_window(w_lora_up, o1, win_w)
    aup = _pad_rows_to_window(a_lora_up, o2, win_a)
    gup = _pad_rows_to_window(g_lora_up, o3, win_g)
    row_spec = lambda w: pl.BlockSpec((1, w), lambda b, i: (0, 0))
    full = lambda a: pl.BlockSpec(a.shape, lambda b, i: (0, 0))
    nt = t // tb
    steps = bsz * nt
    slab = lambda a: pl.BlockSpec((a.shape[0] // steps, a.shape[1]), lambda b, i: (b * nt + i, 0))
    for a in cast:
        assert a.shape[0] % (steps * 2 * SUBLANE) == 0, a.shape
    outs = pl.pallas_call(
        functools.partial(_rwkv_mix_kernel, rw=rw, win_w=win_w, win_a=win_a, win_g=win_g,
                          n_cast=len(cast)),
        grid=(bsz, nt),
        in_specs=[
            pl.BlockSpec((None, tb, cols), lambda b, i: (b, i, 0)),
            row_spec(cols), row_spec(rw), full(wup), row_spec(rw), full(aup), full(gup),
            row_spec(rw), row_spec(rw), row_spec(rw), row_spec(rw), row_spec(rw),
        ] + [slab(a) for a in cast],
        out_specs=[pl.BlockSpec((None, tb, rw), lambda b, i: (b, i, 0))] + [slab(a) for a in cast],
        out_shape=[jax.ShapeDtypeStruct((bsz, t, rw), BF16)]
        + [jax.ShapeDtypeStruct(a.shape, BF16) for a in cast],
        scratch_shapes=[pltpu.VMEM((SUBLANE, cols), F32),
                        pltpu.VMEM((rw // QUAD, QUAD, QUAD), F32)],
        compiler_params=pltpu.CompilerParams(
            dimension_semantics=("parallel", "arbitrary"), vmem_limit_bytes=VMEM_LIMIT),
        name="rwkv_mix",
    )(p, mu, w0, wup, a0, aup, gup, k_k, k_a, r_k, lnx_g, lnx_b, *cast)
    return outs[0], outs[1:]


def _sgu_kernel(z_ref, lng_ref, lnb_ref, ws_ref, bs_ref, o_ref):
    tm, width = o_ref.shape
    u = z_ref[:, :width]
    v = z_ref[:, width:]
    mean = jnp.mean(v, axis=-1, keepdims=True)
    vc = v - mean
    var = jnp.mean(vc * vc, axis=-1, keepdims=True)
    vn = vc * lax.rsqrt(var + LN_EPS) * lng_ref[...] + lnb_ref[...]

    tr = lax.broadcasted_iota(jnp.int32, (SGU_CHUNK, SGU_CHUNK), 0)
    tc = lax.broadcasted_iota(jnp.int32, (SGU_CHUNK, SGU_CHUNK), 1)
    causal = tc <= tr
    bias = bs_ref[...]
    for g in range(width // SGU_GROUP):
        w_c = jnp.where(causal, ws_ref[g], 0.0)
        lanes = slice(g * SGU_GROUP, (g + 1) * SGU_GROUP)
        for n in range(tm // SGU_CHUNK):
            rows = slice(n * SGU_CHUNK, (n + 1) * SGU_CHUNK)
            mixed = _mm(w_c, vn[rows, lanes]) + bias[:, lanes]
            o_ref[rows, lanes] = (u[rows, lanes] * mixed).astype(o_ref.dtype)


def _sgu(z, ln_g, ln_b, w_s, b_full, *, tm):
    m, two_w = z.shape
    width = two_w // 2
    return pl.pallas_call(
        _sgu_kernel,
        grid=(m // tm,),
        in_specs=[
            pl.BlockSpec((tm, two_w), lambda i: (i, 0)),
            pl.BlockSpec((1, width), lambda i: (0, 0)),
            pl.BlockSpec((1, width), lambda i: (0, 0)),
            pl.BlockSpec(w_s.shape, lambda i: (0, 0, 0)),
            pl.BlockSpec(b_full.shape, lambda i: (0, 0)),
        ],
        out_specs=pl.BlockSpec((tm, width), lambda i: (i, 0)),
        out_shape=jax.ShapeDtypeStruct((m, width), BF16),
        compiler_params=pltpu.CompilerParams(
            dimension_semantics=("parallel",), vmem_limit_bytes=VMEM_LIMIT),
        name="sgu",
    )(z, ln_g, ln_b, w_s, b_full)


def _merge_kernel(ya_ref, yb_ref, gates_ref, x_ref, wa_ref, wb_ref, wo_ref, g2_ref, h_ref, n_ref):
    d = x_ref.shape[1]
    merged = (gates_ref[:, :d] * _mm(ya_ref[...], wa_ref[...])
              + gates_ref[:, d:] * _mm(yb_ref[...], wb_ref[...]))
    h = x_ref[...] + _mm(merged.astype(BF16), wo_ref[...])
    h_ref[...] = h
    n_ref[...] = _rmsnorm(h, g2_ref[...]).astype(BF16)


def _merge(ya, yb, gates, x, wa, wb, wo, g2, *, tm):
    m, d = x.shape
    rw = ya.shape[1]
    const = lambda a: pl.BlockSpec(a.shape, lambda i: (0, 0), pipeline_mode=pl.Buffered(1))
    return pl.pallas_call(
        _merge_kernel,
        grid=(m // tm,),
        in_specs=[
            pl.BlockSpec((tm, rw), lambda i: (i, 0)),
            pl.BlockSpec((tm, rw), lambda i: (i, 0)),
            pl.BlockSpec((tm, 2 * d), lambda i: (i, 0)),
            pl.BlockSpec((tm, d), lambda i: (i, 0)),
            const(wa), const(wb), const(wo),
            pl.BlockSpec((1, d), lambda i: (0, 0)),
        ],
        out_specs=[pl.BlockSpec((tm, d), lambda i: (i, 0)), pl.BlockSpec((tm, d), lambda i: (i, 0))],
        out_shape=[jax.ShapeDtypeStruct((m, d), F32), jax.ShapeDtypeStruct((m, d), BF16)],
        compiler_params=pltpu.CompilerParams(
            dimension_semantics=("parallel",), vmem_limit_bytes=VMEM_LIMIT),
        name="merge",
    )(ya, yb, gates, x, wa, wb, wo, g2)


def _ffn_kernel(n_ref, h_ref, wg_ref, wu_ref, wd_ref, gf_ref, o_ref, *, sub):
    j = pl.program_id(1)

    @pl.when(j == 0)
    def _():
        o_ref[...] = h_ref[...]

    n = n_ref[...]
    part = None
    for k in range(wg_ref.shape[1] // sub):
        cols = slice(k * sub, (k + 1) * sub)
        gate = _mm(n, wg_ref[:, cols])
        up = _mm(n, wu_ref[:, cols])
        act = (gate * jax.nn.sigmoid(gate) * up).astype(BF16)
        down = _mm(act, wd_ref[cols, :])
        part = down if part is None else part + down
    o_ref[...] += part

    @pl.when(j == pl.num_programs(1) - 1)
    def _():
        o_ref[...] = _rmsnorm(o_ref[...], gf_ref[...])


def _ffn(n2, h1, wg, wu, wd, gf, *, tm, th, sub=256):
    m, d = h1.shape
    hid = wg.shape[1]
    return pl.pallas_call(
        functools.partial(_ffn_kernel, sub=sub),
        grid=(m // tm, hid // th),
        in_specs=[
            pl.BlockSpec((tm, d), lambda i, j: (i, 0)),
            pl.BlockSpec((tm, d), lambda i, j: (i, 0)),
            pl.BlockSpec((d, th), lambda i, j: (0, j)),
            pl.BlockSpec((d, th), lambda i, j: (0, j)),
            pl.BlockSpec((th, d), lambda i, j: (j, 0)),
            pl.BlockSpec((1, d), lambda i, j: (0, 0)),
        ],
        out_specs=pl.BlockSpec((tm, d), lambda i, j: (i, 0)),
        out_shape=jax.ShapeDtypeStruct((m, d), F32),
        compiler_params=pltpu.CompilerParams(
            dimension_semantics=("parallel", "arbitrary"), vmem_limit_bytes=VMEM_LIMIT),
        name="ffn",
    )(n2, h1, wg, wu, wd, gf)


def _pad_cols(a, width):
    return jnp.pad(a, ((0, 0), (0, width - a.shape[1])))


def _block(h, norm_mix_g, w_in, shift_mu, w0, w_lora_up, a0, a_lora_up, g_lora_up, k_k, k_a, r_k,
           lnx_g, lnx_b, w_proj_rwkv, sgu_ln_g, sgu_ln_b, sgu_w, sgu_b, w_proj_sgu, w_out,
           norm_ffn_g, w_ffn_gate, w_ffn_up, w_ffn_down, g_last, *, bsz, seq):
    m, d = h.shape
    rw = w0.shape[0]
    sw = sgu_ln_g.shape[0]
    rcols = shift_mu.shape[0]
    zc = rcols + 2 * sw
    rcols_al = -(-rcols // LANE) * LANE
    row = lambda a: a.reshape(1, -1)

    w_t = w_in.T
    mu = _pad_cols(shift_mu[None, :], rcols_al)

    n1 = _rmsnorm_call(h, row(norm_mix_g), tm=512)
    p = _proj(n1, w_t, 0, rcols_al, tm=1024, tn=rcols_al // 2, epilogue="none", w_buffers=1)
    z = _proj(n1, w_t, rcols, 2 * sw, tm=2048, tn=1024, epilogue="gelu")
    gates = _proj(n1, w_t, zc, 2 * d, tm=2048, tn=1024, epilogue="sigmoid")

    ya, (wa, wb, wo, wg, wu, wd) = _rwkv_mix(
        p.reshape(bsz, seq, -1), mu, row(w0), w_lora_up, row(a0), a_lora_up, g_lora_up,
        row(k_k), row(k_a), row(r_k), row(lnx_g), row(lnx_b),
        (w_proj_rwkv, w_proj_sgu, w_out, w_ffn_gate, w_ffn_up, w_ffn_down), tb=256)

    b_full = jnp.repeat(sgu_b.T, SGU_GROUP, axis=1)
    yb = _sgu(z, row(sgu_ln_g), row(sgu_ln_b), sgu_w, b_full, tm=512)

    h1, n2 = _merge(ya.reshape(m, rw), yb, gates, h, wa, wb, wo, row(norm_ffn_g), tm=256)
    return _ffn(n2, h1, wg, wu, wd, row(g_last), tm=1024, th=512)


def kernel(x, norm_mix_g, w_in, shift_mu, w0, w_lora_up, a0, a_lora_up, g_lora_up, k_k, k_a, r_k,
           lnx_g, lnx_b, w_proj_rwkv, sgu_ln_g, sgu_ln_b, sgu_w, sgu_b, w_proj_sgu, w_out,
           norm_ffn_g, w_ffn_gate, w_ffn_up, w_ffn_down, norm_final_g):
    bsz, seq, d = x.shape
    depth = w_in.shape[0]
    assert depth == 1, "the final rmsnorm is fused into the (single) layer's channel mixer"
    h = x.reshape(bsz * seq, d)
    out = _block(h, norm_mix_g[0], w_in[0], shift_mu[0], w0[0], w_lora_up[0], a0[0], a_lora_up[0],
                 g_lora_up[0], k_k[0], k_a[0], r_k[0], lnx_g[0], lnx_b[0], w_proj_rwkv[0],
                 sgu_ln_g[0], sgu_ln_b[0], sgu_w[0], sgu_b[0], w_proj_sgu[0], w_out[0],
                 norm_ffn_g[0], w_ffn_gate[0], w_ffn_up[0], w_ffn_down[0], norm_final_g,
                 bsz=bsz, seq=seq)
    return out.reshape(bsz, seq, d)
```

```python
import functools

import jax
import jax.numpy as jnp
from jax import lax
from jax.experimental import pallas as pl
from jax.experimental.pallas import tpu as pltpu

F32 = jnp.float32
BF16 = jnp.bfloat16

RMS_EPS = 1e-6
LN_EPS = 1e-5
LNX_EPS = 64e-5

HEAD = 64
WKV_CHUNK = 64
QUAD = 256
HEADS_PER_QUAD = QUAD // HEAD
SGU_CHUNK = 128
SGU_GROUP = 128
LANE = 128
SUBLANE = 8

VMEM_LIMIT = 60 * 1024 * 1024


def _mm(a, b):
    return jnp.dot(a, b, preferred_element_type=F32)


def _mm_nt(a, b):
    return lax.dot_general(a, b, (((1,), (1,)), ((), ())), preferred_element_type=F32)


def _mm_tn(a, b):
    return lax.dot_general(a, b, (((0,), (0,)), ((), ())), preferred_element_type=F32)


def _rmsnorm(x, g):
    return x * lax.rsqrt(jnp.mean(x * x, axis=-1, keepdims=True) + RMS_EPS) * g


def _rmsnorm_kernel(x_ref, g_ref, o_ref):
    o_ref[...] = _rmsnorm(x_ref[...], g_ref[...]).astype(o_ref.dtype)


def _rmsnorm_call(x, g, *, tm):
    m, d = x.shape
    return pl.pallas_call(
        _rmsnorm_kernel,
        grid=(m // tm,),
        in_specs=[pl.BlockSpec((tm, d), lambda i: (i, 0)), pl.BlockSpec((1, d), lambda i: (0, 0))],
        out_specs=pl.BlockSpec((tm, d), lambda i: (i, 0)),
        out_shape=jax.ShapeDtypeStruct((m, d), BF16),
        compiler_params=pltpu.CompilerParams(dimension_semantics=("parallel",)),
        name="rmsnorm",
    )(x, g)


def _proj_kernel(n_ref, wt_ref, o_ref, w16_ref, *, epilogue, sub):
    @pl.when(pl.program_id(1) == 0)
    def _():
        w16_ref[...] = wt_ref[...].astype(BF16)

    for k in range(o_ref.shape[1] // sub):
        cols = slice(k * sub, (k + 1) * sub)
        y = _mm_nt(n_ref[...], w16_ref[cols, :])
        if epilogue == "gelu":
            y = 0.5 * y * (1.0 + lax.erf(y * (2.0 ** -0.5)))
        elif epilogue == "sigmoid":
            y = jax.nn.sigmoid(y)
        o_ref[:, cols] = y.astype(o_ref.dtype)


def _proj(n, w_t, row0, n_cols, *, tm, tn, epilogue, sub=256, w_buffers=2):
    m, d = n.shape
    return pl.pallas_call(
        functools.partial(_proj_kernel, epilogue=epilogue, sub=sub),
        grid=(n_cols // tn, m // tm),
        in_specs=[
            pl.BlockSpec((tm, d), lambda j, i: (i, 0)),
            pl.BlockSpec((pl.Element(tn), pl.Element(d)),
                         lambda j, i: (pl.multiple_of(row0 + j * tn, SUBLANE), 0),
                         pipeline_mode=pl.Buffered(w_buffers)),
        ],
        out_specs=pl.BlockSpec((tm, tn), lambda j, i: (i, j)),
        out_shape=jax.ShapeDtypeStruct((m, n_cols), F32),
        scratch_shapes=[pltpu.VMEM((tn, d), BF16)],
        compiler_params=pltpu.CompilerParams(
            dimension_semantics=("parallel", "arbitrary"), vmem_limit_bytes=VMEM_LIMIT),
        name="proj_" + epilogue,
    )(n, w_t)


def _rwkv_mix_kernel(p_ref, mu_ref, w0_ref, wup_ref, a0_ref, aup_ref, gup_ref, kk_ref, ka_ref,
                     rk_ref, lng_ref, lnb_ref, *rest, rw, win_w, win_a, win_g, n_cast):
    cast_in, o_ref, cast_out = rest[:n_cast], rest[n_cast], rest[n_cast + 1:2 * n_cast + 1]
    carry_ref, state_ref = rest[2 * n_cast + 1:]
    for src_ref, dst_ref in zip(cast_in, cast_out):
        dst_ref[...] = src_ref[...].astype(dst_ref.dtype)

    tb = p_ref.shape[0]
    c_sz = WKV_CHUNK
    n_quads = rw // QUAD
    qs = range(n_quads)
    lanes = [slice(q * QUAD, (q + 1) * QUAD) for q in qs]
    rows = HEADS_PER_QUAD * c_sz
    assert rows == QUAD

    @pl.when(pl.program_id(1) == 0)
    def _():
        carry_ref[...] = jnp.zeros_like(carry_ref)
        state_ref[...] = jnp.zeros_like(state_ref)

    ri = lax.broadcasted_iota(jnp.int32, (rows, QUAD), 0)
    ci = lax.broadcasted_iota(jnp.int32, (rows, QUAD), 1)
    ones_bd = jnp.where((ri // HEAD) == (ci // HEAD), 1.0, 0.0).astype(F32)
    avg_bd = ones_bd * (1.0 / HEAD)
    head_mask = (ri // c_sz) == (ci // HEAD)
    same = (ri // c_sz) == (ci // c_sz)
    strict = same & (ci < ri)
    incl = same & (ci <= ri)
    leaf = (ri // 16) == (ci // 16)
    in32 = (ri // 32) == (ci // 32)
    eye = ri == ci
    eye_f = jnp.where(eye, 1.0, 0.0).astype(F32)
    tr = lax.broadcasted_iota(jnp.int32, (c_sz, c_sz), 0)
    tc = lax.broadcasted_iota(jnp.int32, (c_sz, c_sz), 1)
    tri = jnp.where(tc <= tr, 1.0, 0.0).astype(BF16)
    crow = lax.broadcasted_iota(jnp.int32, (c_sz, 1), 0)

    def head_sums(xs):
        lhs = jnp.concatenate([x[:, lanes[q]] for x in xs for q in qs], axis=0)
        out = _mm(lhs, ones_bd)
        return [jnp.concatenate([out[(i * n_quads + q) * c_sz:(i * n_quads + q + 1) * c_sz]
                                 for q in qs], axis=1) for i in range(len(xs))]

    def b16(x):
        return x.astype(BF16)

    def stack(x):
        return jnp.where(head_mask, jnp.concatenate([b16(x)] * HEADS_PER_QUAD, axis=0), 0.0)

    def expand(x):
        return jnp.concatenate([x, x], axis=1)

    def prep_chunk(c):
        r0 = c * c_sz
        p = p_ref[r0:r0 + c_sz, :]
        before = carry_ref[0:1, :] if c == 0 else p_ref[r0 - 1:r0, :]
        prev = jnp.where(crow == 0, before, pltpu.roll(p, 1, 0))
        sh = p + (prev - p) * mu_ref[...]
        r = sh[:, 0:rw]
        k = sh[:, rw:2 * rw]
        v = sh[:, 2 * rw:3 * rw]
        xw = sh[:, win_w[0]:win_w[1]]
        xa = sh[:, win_a[0]:win_a[1]]
        xg = sh[:, win_g[0]:win_g[1]]

        wl = w0_ref[...] + _mm(jnp.tanh(xw), wup_ref[...])
        lw = (-(2.718281828459045 ** -0.5)) * jax.nn.sigmoid(wl)
        a = jax.nn.sigmoid(a0_ref[...] + _mm(xa, aup_ref[...]))
        g = _mm(jax.nn.sigmoid(xg), gup_ref[...])

        kk = k * kk_ref[...]
        k2 = k * (1.0 + (a - 1.0) * ka_ref[...])
        kk_ss, rk_sum = head_sums([kk * kk, r * k2 * rk_ref[...]])
        kk = kk * jnp.minimum(lax.rsqrt(kk_ss), 1e12)
        bonus = rk_sum * v

        h1 = lw.astype(BF16)
        r1 = lw - h1.astype(F32)
        h2 = r1.astype(BF16)
        h3 = (r1 - h2.astype(F32)).astype(BF16)
        cum = _mm(tri, h1) + _mm(tri, h2) + _mm(tri, h3)
        ec = jnp.exp(cum)
        eci = jnp.exp(-cum)
        ec_prev = jnp.where(crow == 0, 1.0, pltpu.roll(ec, 1, 0))
        return dict(ah=-kk * ec_prev, rh=r * ec, bh=kk * a * eci, kh=k2 * eci, v=v,
                    pc=ec[c_sz - 1:c_sz, :], bonus=bonus, g=g)

    def wkv_chunk(d):
        ah = [d["ah"][:, lanes[q]] for q in qs]
        rh = [d["rh"][:, lanes[q]] for q in qs]
        bh = [d["bh"][:, lanes[q]] for q in qs]
        kh = [d["kh"][:, lanes[q]] for q in qs]
        vv = [d["v"][:, lanes[q]] for q in qs]
        pc = [d["pc"][:, lanes[q]] for q in qs]

        ah_s = [stack(x) for x in ah]
        rh_s = [stack(x) for x in rh]
        v_s = [stack(x) for x in vv]
        s = [_mm_nt(jnp.concatenate([ah_s[q], rh_s[q]], axis=0),
                    b16(jnp.concatenate([bh[q], bh[q], kh[q], kh[q]], axis=0))) for q in qs]
        a_ab = [jnp.where(strict, expand(s[q][:rows, :QUAD // 2]), 0.0) for q in qs]
        a_ak = [b16(jnp.where(strict, expand(s[q][:rows, QUAD // 2:]), 0.0)) for q in qs]
        a_rb = [b16(jnp.where(incl, expand(s[q][rows:, :QUAD // 2]), 0.0)) for q in qs]
        a_rk = [b16(jnp.where(incl, expand(s[q][rows:, QUAD // 2:]), 0.0)) for q in qs]

        a_d = [jnp.where(leaf, a_ab[q], 0.0) for q in qs]
        t_inv = [eye_f + a_d[q] for q in qs]
        pw = [b16(a_d[q]) for q in qs]
        for _ in range(3):
            pw = [b16(_mm(pw[q], pw[q])) for q in qs]
            t_inv = [t_inv[q] + _mm(pw[q], b16(t_inv[q])) for q in qs]
        for a_off in ([b16(jnp.where(in32 & jnp.logical_not(leaf), a_ab[q], 0.0)) for q in qs],
                      [b16(jnp.where(in32, 0.0, a_ab[q])) for q in qs]):
            t16 = [b16(t_inv[q]) for q in qs]
            ta = [b16(_mm(t16[q], a_off[q])) for q in qs]
            t_inv = [t_inv[q] + _mm(ta[q], t16[q]) for q in qs]

        akv = [b16(_mm(a_ak[q], v_s[q])) for q in qs]
        wu = [_mm(b16(t_inv[q]), jnp.concatenate([ah_s[q], akv[q]], axis=1)) for q in qs]
        wt = [b16(wu[q][:, :QUAD]) for q in qs]
        ut = [wu[q][:, QUAD:] for q in qs]
        bk = [jnp.concatenate([stack(bh[q] * pc[q]), stack(kh[q] * pc[q])], axis=0) for q in qs]
        pc_col = [jnp.sum(jnp.where(eye, pc[q], 0.0), axis=1, keepdims=True) for q in qs]
        return dict(rh_s=rh_s, v_s=v_s, a_rb=a_rb, a_rk=a_rk, wt=wt, ut=ut, bk=bk, pc_col=pc_col)

    def wkv_state(w):
        rh_s, v_s, a_rb, a_rk = w["rh_s"], w["v_s"], w["a_rb"], w["a_rk"]
        wt, ut, bk, pc_col = w["wt"], w["ut"], w["bk"], w["pc_col"]
        m_st = [state_ref[q] for q in qs]
        m16 = [b16(m_st[q]) for q in qs]
        u16 = [b16(_mm(wt[q], m16[q]) + ut[q]) for q in qs]
        muv = [jnp.concatenate([m16[q], u16[q], v_s[q]], axis=0) for q in qs]
        y_s = [_mm(jnp.concatenate([rh_s[q], a_rb[q], a_rk[q]], axis=1), muv[q]) for q in qs]
        ys = []
        for q in qs:
            y = y_s[q][0:c_sz]
            for h in range(1, HEADS_PER_QUAD):
                y = y + y_s[q][h * c_sz:(h + 1) * c_sz]
            ys.append(y)
            state_ref[q] = (pc_col[q] * m_st[q]
                            + _mm_tn(bk[q], jnp.concatenate([u16[q], v_s[q]], axis=0)))
        return ys

    def post_chunk(c, ys, d):
        y = jnp.concatenate(ys, axis=0)
        yc = y - _mm(y, avg_bd)
        var = _mm(yc * yc, avg_bd)
        yn = yc * lax.rsqrt(var + LNX_EPS)
        yn = jnp.concatenate([yn[q * c_sz:(q + 1) * c_sz] for q in qs], axis=1)
        out = (yn * lng_ref[...] + lnb_ref[...] + d["bonus"]) * d["g"]
        o_ref[c * c_sz:(c + 1) * c_sz, :] = out.astype(o_ref.dtype)

    n_chunks = tb // c_sz
    nxt = prep_chunk(0)
    nxt_w = wkv_chunk(nxt)
    for c in range(n_chunks):
        cur, cur_w = nxt, nxt_w
        if c + 1 < n_chunks:
            nxt = prep_chunk(c + 1)
            nxt_w = wkv_chunk(nxt)
        post_chunk(c, wkv_state(cur_w), cur)
    carry_ref[0:1, :] = p_ref[tb - 1:tb, :]


def _lane_window(start, stop):
    return (start // LANE) * LANE, -(-stop // LANE) * LANE


def _pad_rows_to_window(w, start, window):
    return jnp.pad(w, ((start - window[0], window[1] - start - w.shape[0]), (0, 0)))


def _rwkv_mix(p, mu, w0, w_lora_up, a0, a_lora_up, g_lora_up, k_k, k_a, r_k, lnx_g, lnx_b, cast, *,
              tb):
    bsz, t, cols = p.shape
    rw = w0.shape[1]
    o1 = 3 * rw
    o2 = o1 + w_lora_up.shape[0]
    o3 = o2 + a_lora_up.shape[0]
    o4 = o3 + g_lora_up.shape[0]
    win_w, win_a, win_g = _lane_window(o1, o2), _lane_window(o2, o3), _lane_window(o3, o4)
    assert win_g[1] == cols
    wup = _pad_rows_to_window(w_lora_up, o1, win_w)
    aup = _pad_rows_to_window(a_lora_up, o2, win_a)
    gup = _pad_rows_to_window(g_lora_up, o3, win_g)
    row_spec = lambda w: pl.BlockSpec((1, w), lambda b, i: (0, 0))
    full = lambda a: pl.BlockSpec(a.shape, lambda b, i: (0, 0))
    nt = t // tb
    steps = bsz * nt
    def slab_in(a, row0, nrows):
        r = nrows // steps
        return pl.BlockSpec((pl.Element(r), pl.Element(a.shape[1])),
                            lambda b, i: (pl.multiple_of(row0 + (b * nt + i) * r, SUBLANE), 0))

    def slab_out(a, row0, nrows):
        return pl.BlockSpec((nrows // steps, a.shape[1]), lambda b, i: (b * nt + i, 0))

    for a, row0, nrows in cast:
        assert row0 % SUBLANE == 0 and nrows % (steps * 2 * SUBLANE) == 0
    outs = pl.pallas_call(
        functools.partial(_rwkv_mix_kernel, rw=rw, win_w=win_w, win_a=win_a, win_g=win_g,
                          n_cast=len(cast)),
        grid=(bsz, nt),
        in_specs=[
            pl.BlockSpec((None, tb, cols), lambda b, i: (b, i, 0)),
            row_spec(cols), row_spec(rw), full(wup), row_spec(rw), full(aup), full(gup),
            row_spec(rw), row_spec(rw), row_spec(rw), row_spec(rw), row_spec(rw),
        ] + [slab_in(*c) for c in cast],
        out_specs=[pl.BlockSpec((None, tb, rw), lambda b, i: (b, i, 0))]
        + [slab_out(*c) for c in cast],
        out_shape=[jax.ShapeDtypeStruct((bsz, t, rw), BF16)]
        + [jax.ShapeDtypeStruct((nrows, a.shape[1]), BF16) for a, _, nrows in cast],
        scratch_shapes=[pltpu.VMEM((SUBLANE, cols), F32),
                        pltpu.VMEM((rw // QUAD, QUAD, QUAD), F32)],
        compiler_params=pltpu.CompilerParams(
            dimension_semantics=("parallel", "arbitrary"), vmem_limit_bytes=VMEM_LIMIT),
        name="rwkv_mix",
    )(p, mu, w0, wup, a0, aup, gup, k_k, k_a, r_k, lnx_g, lnx_b, *[a for a, _, _ in cast])
    return outs[0], outs[1:]


def _sgu_kernel(z_ref, lng_ref, lnb_ref, ws_ref, bs_ref, o_ref):
    tm, width = o_ref.shape
    u = z_ref[:, :width]
    v = z_ref[:, width:]
    mean = jnp.mean(v, axis=-1, keepdims=True)
    vc = v - mean
    var = jnp.mean(vc * vc, axis=-1, keepdims=True)
    vn = vc * lax.rsqrt(var + LN_EPS) * lng_ref[...] + lnb_ref[...]

    tr = lax.broadcasted_iota(jnp.int32, (SGU_CHUNK, SGU_CHUNK), 0)
    tc = lax.broadcasted_iota(jnp.int32, (SGU_CHUNK, SGU_CHUNK), 1)
    causal = tc <= tr
    bias = bs_ref[...]
    for g in range(width // SGU_GROUP):
        w_c = jnp.where(causal, ws_ref[g], 0.0)
        lanes = slice(g * SGU_GROUP, (g + 1) * SGU_GROUP)
        for n in range(tm // SGU_CHUNK):
            rows = slice(n * SGU_CHUNK, (n + 1) * SGU_CHUNK)
            mixed = _mm(w_c, vn[rows, lanes]) + bias[:, lanes]
            o_ref[rows, lanes] = (u[rows, lanes] * mixed).astype(o_ref.dtype)


def _sgu(z, ln_g, ln_b, w_s, b_full, *, tm):
    m, two_w = z.shape
    width = two_w // 2
    return pl.pallas_call(
        _sgu_kernel,
        grid=(m // tm,),
        in_specs=[
            pl.BlockSpec((tm, two_w), lambda i: (i, 0)),
            pl.BlockSpec((1, width), lambda i: (0, 0)),
            pl.BlockSpec((1, width), lambda i: (0, 0)),
            pl.BlockSpec(w_s.shape, lambda i: (0, 0, 0)),
            pl.BlockSpec(b_full.shape, lambda i: (0, 0)),
        ],
        out_specs=pl.BlockSpec((tm, width), lambda i: (i, 0)),
        out_shape=jax.ShapeDtypeStruct((m, width), BF16),
        compiler_params=pltpu.CompilerParams(
            dimension_semantics=("parallel",), vmem_limit_bytes=VMEM_LIMIT),
        name="sgu",
    )(z, ln_g, ln_b, w_s, b_full)


def _merge_kernel(n1_ref, ya_ref, yb_ref, x_ref, wgt_ref, wa_ref, wb_ref, wo_ref, g2_ref, h_ref,
                  n_ref, *, sub):
    d = x_ref.shape[1]
    n1 = n1_ref[...]
    ya = ya_ref[...]
    yb = yb_ref[...]
    parts = []
    for k in range(d // sub):
        cols = slice(k * sub, (k + 1) * sub)
        ga = jax.nn.sigmoid(_mm_nt(n1, wgt_ref[k * sub:(k + 1) * sub, :]))
        gb = jax.nn.sigmoid(_mm_nt(n1, wgt_ref[d + k * sub:d + (k + 1) * sub, :]))
        parts.append((ga * _mm(ya, wa_ref[:, cols]) + gb * _mm(yb, wb_ref[:, cols])).astype(BF16))
    h = x_ref[...] + _mm(jnp.concatenate(parts, axis=1), wo_ref[...])
    h_ref[...] = h
    n_ref[...] = _rmsnorm(h, g2_ref[...]).astype(BF16)


def _merge(n1, ya, yb, x, wgt, wa, wb, wo, g2, *, tm, sub=256):
    m, d = x.shape
    rw = ya.shape[1]
    const = lambda a: pl.BlockSpec(a.shape, lambda i: (0, 0), pipeline_mode=pl.Buffered(1))
    return pl.pallas_call(
        functools.partial(_merge_kernel, sub=sub),
        grid=(m // tm,),
        in_specs=[
            pl.BlockSpec((tm, d), lambda i: (i, 0)),
            pl.BlockSpec((tm, rw), lambda i: (i, 0)),
            pl.BlockSpec((tm, rw), lambda i: (i, 0)),
            pl.BlockSpec((tm, d), lambda i: (i, 0)),
            const(wgt), const(wa), const(wb), const(wo),
            pl.BlockSpec((1, d), lambda i: (0, 0)),
        ],
        out_specs=[pl.BlockSpec((tm, d), lambda i: (i, 0)), pl.BlockSpec((tm, d), lambda i: (i, 0))],
        out_shape=[jax.ShapeDtypeStruct((m, d), F32), jax.ShapeDtypeStruct((m, d), BF16)],
        compiler_params=pltpu.CompilerParams(
            dimension_semantics=("parallel",), vmem_limit_bytes=VMEM_LIMIT),
        name="merge",
    )(n1, ya, yb, x, wgt, wa, wb, wo, g2)


def _ffn_kernel(n_ref, h_ref, wg_ref, wu_ref, wd_ref, gf_ref, o_ref, *, sub):
    j = pl.program_id(1)

    @pl.when(j == 0)
    def _():
        o_ref[...] = h_ref[...]

    n = n_ref[...]
    part = None
    for k in range(wg_ref.shape[1] // sub):
        cols = slice(k * sub, (k + 1) * sub)
        gate = _mm(n, wg_ref[:, cols])
        up = _mm(n, wu_ref[:, cols])
        act = (gate * jax.nn.sigmoid(gate) * up).astype(BF16)
        down = _mm(act, wd_ref[cols, :])
        part = down if part is None else part + down
    o_ref[...] += part

    @pl.when(j == pl.num_programs(1) - 1)
    def _():
        o_ref[...] = _rmsnorm(o_ref[...], gf_ref[...])


def _ffn(n2, h1, wg, wu, wd, gf, *, tm, th, sub=256):
    m, d = h1.shape
    hid = wg.shape[1]
    return pl.pallas_call(
        functools.partial(_ffn_kernel, sub=sub),
        grid=(m // tm, hid // th),
        in_specs=[
            pl.BlockSpec((tm, d), lambda i, j: (i, 0)),
            pl.BlockSpec((tm, d), lambda i, j: (i, 0)),
            pl.BlockSpec((d, th), lambda i, j: (0, j)),
            pl.BlockSpec((d, th), lambda i, j: (0, j)),
            pl.BlockSpec((th, d), lambda i, j: (j, 0)),
            pl.BlockSpec((1, d), lambda i, j: (0, 0)),
        ],
        out_specs=pl.BlockSpec((tm, d), lambda i, j: (i, 0)),
        out_shape=jax.ShapeDtypeStruct((m, d), F32),
        compiler_params=pltpu.CompilerParams(
            dimension_semantics=("parallel", "arbitrary"), vmem_limit_bytes=VMEM_LIMIT),
        name="ffn",
    )(n2, h1, wg, wu, wd, gf)


def _pad_cols(a, width):
    return jnp.pad(a, ((0, 0), (0, width - a.shape[1])))


def _block(h, norm_mix_g, w_in, shift_mu, w0, w_lora_up, a0, a_lora_up, g_lora_up, k_k, k_a, r_k,
           lnx_g, lnx_b, w_proj_rwkv, sgu_ln_g, sgu_ln_b, sgu_w, sgu_b, w_proj_sgu, w_out,
           norm_ffn_g, w_ffn_gate, w_ffn_up, w_ffn_down, g_last, *, bsz, seq):
    m, d = h.shape
    rw = w0.shape[0]
    sw = sgu_ln_g.shape[0]
    rcols = shift_mu.shape[0]
    zc = rcols + 2 * sw
    rcols_al = -(-rcols // LANE) * LANE
    row = lambda a: a.reshape(1, -1)

    w_t = w_in.T
    mu = _pad_cols(shift_mu[None, :], rcols_al)

    n1 = _rmsnorm_call(h, row(norm_mix_g), tm=512)
    p = _proj(n1, w_t, 0, rcols_al, tm=1024, tn=rcols_al // 2, epilogue="none", w_buffers=1)
    z = _proj(n1, w_t, rcols, 2 * sw, tm=2048, tn=1024, epilogue="gelu")

    whole = lambda a: (a, 0, a.shape[0])
    ya, (wgt, wa, wb, wo, wg, wu, wd) = _rwkv_mix(
        p.reshape(bsz, seq, -1), mu, row(w0), w_lora_up, row(a0), a_lora_up, g_lora_up,
        row(k_k), row(k_a), row(r_k), row(lnx_g), row(lnx_b),
        ((w_t, zc, 2 * d), whole(w_proj_rwkv), whole(w_proj_sgu), whole(w_out),
         whole(w_ffn_gate), whole(w_ffn_up), whole(w_ffn_down)), tb=256)

    b_full = jnp.repeat(sgu_b.T, SGU_GROUP, axis=1)
    yb = _sgu(z, row(sgu_ln_g), row(sgu_ln_b), sgu_w, b_full, tm=512)

    h1, n2 = _merge(n1, ya.reshape(m, rw), yb, h, wgt, wa, wb, wo, row(norm_ffn_g), tm=256)
    return _ffn(n2, h1, wg, wu, wd, row(g_last), tm=1024, th=512)


def kernel(x, norm_mix_g, w_in, shift_mu, w0, w_lora_up, a0, a_lora_up, g_lora_up, k_k, k_a, r_k,
           lnx_g, lnx_b, w_proj_rwkv, sgu_ln_g, sgu_ln_b, sgu_w, sgu_b, w_proj_sgu, w_out,
           norm_ffn_g, w_ffn_gate, w_ffn_up, w_ffn_down, norm_final_g):
    bsz, seq, d = x.shape
    depth = w_in.shape[0]
    assert depth == 1, "the final rmsnorm is fused into the (single) layer's channel mixer"
    h = x.reshape(bsz * seq, d)
    out = _block(h, norm_mix_g[0], w_in[0], shift_mu[0], w0[0], w_lora_up[0], a0[0], a_lora_up[0],
                 g_lora_up[0], k_k[0], k_a[0], r_k[0], lnx_g[0], lnx_b[0], w_proj_rwkv[0],
                 sgu_ln_g[0], sgu_ln_b[0], sgu_w[0], sgu_b[0], w_proj_sgu[0], w_out[0],
                 norm_ffn_g[0], w_ffn_gate[0], w_ffn_up[0], w_ffn_down[0], norm_final_g,
                 bsz=bsz, seq=seq)
    return out.reshape(bsz, seq, d)
```

```python
import functools

import jax
import jax.numpy as jnp
from jax import lax
from jax.experimental import pallas as pl
from jax.experimental.pallas import tpu as pltpu

F32 = jnp.float32
BF16 = jnp.bfloat16

RMS_EPS = 1e-6
LN_EPS = 1e-5
LNX_EPS = 64e-5

HEAD = 64
WKV_CHUNK = 64
QUAD = 256
HEADS_PER_QUAD = QUAD // HEAD
SGU_CHUNK = 128
SGU_GROUP = 128
LANE = 128
SUBLANE = 8

VMEM_LIMIT = 60 * 1024 * 1024


def _mm(a, b):
    return jnp.dot(a, b, preferred_element_type=F32)


def _mm_nt(a, b):
    return lax.dot_general(a, b, (((1,), (1,)), ((), ())), preferred_element_type=F32)


def _rmsnorm(x, g):
    return x * lax.rsqrt(jnp.mean(x * x, axis=-1, keepdims=True) + RMS_EPS) * g


def _proj_kernel(n_ref, w16t_ref, o_ref, *, sub):
    for k in range(o_ref.shape[1] // sub):
        cols = slice(k * sub, (k + 1) * sub)
        o_ref[:, cols] = _mm_nt(n_ref[...], w16t_ref[cols, :])


def _proj(n, w16t, *, tm, sub=256):
    m, d = n.shape
    n_cols = w16t.shape[0]
    return pl.pallas_call(
        functools.partial(_proj_kernel, sub=sub),
        grid=(m // tm,),
        in_specs=[
            pl.BlockSpec((tm, d), lambda i: (i, 0)),
            pl.BlockSpec((n_cols, d), lambda i: (0, 0), pipeline_mode=pl.Buffered(1)),
        ],
        out_specs=pl.BlockSpec((tm, n_cols), lambda i: (i, 0)),
        out_shape=jax.ShapeDtypeStruct((m, n_cols), F32),
        compiler_params=pltpu.CompilerParams(
            dimension_semantics=("parallel",), vmem_limit_bytes=VMEM_LIMIT),
        name="proj",
    )(n, w16t)


def _slab_specs(a, row0, nrows, steps, step_index):
    assert row0 % SUBLANE == 0 and nrows % (steps * 2 * SUBLANE) == 0
    r = nrows // steps
    cols = a.shape[1]
    in_spec = pl.BlockSpec(
        (pl.Element(r), pl.Element(cols)),
        lambda *g: (pl.multiple_of(row0 + step_index(*g) * r, SUBLANE), 0))
    out_spec = pl.BlockSpec((r, cols), lambda *g: (step_index(*g), 0))
    return in_spec, out_spec, jax.ShapeDtypeStruct((nrows, cols), BF16)


def _rwkv_mix_kernel(p_ref, mu_ref, w0_ref, wup_ref, a0_ref, aup_ref, gup_ref, kk_ref, ka_ref,
                     rk_ref, lng_ref, lnb_ref, *rest, rw, win_w, win_a, win_g, n_cast):
    cast_in, o_ref, cast_out = rest[:n_cast], rest[n_cast], rest[n_cast + 1:2 * n_cast + 1]
    carry_ref, state_ref = rest[2 * n_cast + 1:]
    for src_ref, dst_ref in zip(cast_in, cast_out):
        dst_ref[...] = src_ref[...].astype(dst_ref.dtype)

    tb = p_ref.shape[0]
    c_sz = WKV_CHUNK
    n_quads = rw // QUAD
    qs = range(n_quads)
    lanes = [slice(q * QUAD, (q + 1) * QUAD) for q in qs]
    rows = HEADS_PER_QUAD * c_sz
    assert rows == QUAD

    @pl.when(pl.program_id(1) == 0)
    def _():
        carry_ref[...] = jnp.zeros_like(carry_ref)
        state_ref[...] = jnp.zeros_like(state_ref)

    ri = lax.broadcasted_iota(jnp.int32, (rows, QUAD), 0)
    ci = lax.broadcasted_iota(jnp.int32, (rows, QUAD), 1)
    ones_bd = jnp.where((ri // HEAD) == (ci // HEAD), 1.0, 0.0).astype(F32)
    avg_bd = ones_bd * (1.0 / HEAD)
    head_mask = (ri // c_sz) == (ci // HEAD)
    same = (ri // c_sz) == (ci // c_sz)
    strict = same & (ci < ri)
    incl = same & (ci <= ri)
    leaf = (ri // 16) == (ci // 16)
    in32 = (ri // 32) == (ci // 32)
    eye = ri == ci
    eye_f = jnp.where(eye, 1.0, 0.0).astype(F32)
    tr = lax.broadcasted_iota(jnp.int32, (c_sz, c_sz), 0)
    tc = lax.broadcasted_iota(jnp.int32, (c_sz, c_sz), 1)
    tri = jnp.where(tc <= tr, 1.0, 0.0).astype(BF16)
    crow = lax.broadcasted_iota(jnp.int32, (c_sz, 1), 0)

    def head_sums(xs):
        lhs = jnp.concatenate([x[:, lanes[q]] for x in xs for q in qs], axis=0)
        out = _mm(lhs, ones_bd)
        return [jnp.concatenate([out[(i * n_quads + q) * c_sz:(i * n_quads + q + 1) * c_sz]
                                 for q in qs], axis=1) for i in range(len(xs))]

    def b16(x):
        return x.astype(BF16)

    def stack(x):
        return jnp.where(head_mask, jnp.concatenate([b16(x)] * HEADS_PER_QUAD, axis=0), 0.0)

    def expand(x):
        return jnp.concatenate([x, x], axis=1)

    def prep_chunk(c):
        r0 = c * c_sz
        p = p_ref[r0:r0 + c_sz, :]
        before = carry_ref[0:1, :] if c == 0 else p_ref[r0 - 1:r0, :]
        prev = jnp.where(crow == 0, before, pltpu.roll(p, 1, 0))
        sh = p + (prev - p) * mu_ref[...]
        r = sh[:, 0:rw]
        k = sh[:, rw:2 * rw]
        v = sh[:, 2 * rw:3 * rw]
        xw = sh[:, win_w[0]:win_w[1]]
        xa = sh[:, win_a[0]:win_a[1]]
        xg = sh[:, win_g[0]:win_g[1]]

        wl = w0_ref[...] + _mm(jnp.tanh(xw), wup_ref[...])
        lw = (-(2.718281828459045 ** -0.5)) * jax.nn.sigmoid(wl)
        a = jax.nn.sigmoid(a0_ref[...] + _mm(xa, aup_ref[...]))
        g = _mm(jax.nn.sigmoid(xg), gup_ref[...])

        kk = k * kk_ref[...]
        k2 = k * (1.0 + (a - 1.0) * ka_ref[...])
        kk_ss, rk_sum = head_sums([kk * kk, r * k2 * rk_ref[...]])
        kk = kk * jnp.minimum(lax.rsqrt(kk_ss), 1e12)
        bonus = rk_sum * v

        h1 = lw.astype(BF16)
        r1 = lw - h1.astype(F32)
        h2 = r1.astype(BF16)
        h3 = (r1 - h2.astype(F32)).astype(BF16)
        cum = _mm(tri, h1) + _mm(tri, h2) + _mm(tri, h3)
        ec = jnp.exp(cum)
        eci = jnp.exp(-cum)
        ec_prev = jnp.where(crow == 0, 1.0, pltpu.roll(ec, 1, 0))
        return dict(ah=-kk * ec_prev, rh=r * ec, bh=kk * a * eci, kh=k2 * eci, v=v,
                    pc=ec[c_sz - 1:c_sz, :], bonus=bonus, g=g)

    def wkv_chunk(d):
        ah = [d["ah"][:, lanes[q]] for q in qs]
        rh = [d["rh"][:, lanes[q]] for q in qs]
        bh = [d["bh"][:, lanes[q]] for q in qs]
        kh = [d["kh"][:, lanes[q]] for q in qs]
        vv = [d["v"][:, lanes[q]] for q in qs]
        pc = [d["pc"][:, lanes[q]] for q in qs]

        ah_s = [stack(x) for x in ah]
        rh_s = [stack(x) for x in rh]
        v_s = [stack(x) for x in vv]
        s = [_mm_nt(jnp.concatenate([ah_s[q], rh_s[q]], axis=0),
                    b16(jnp.concatenate([bh[q], bh[q], kh[q], kh[q]], axis=0))) for q in qs]
        a_ab = [jnp.where(strict, expand(s[q][:rows, :QUAD // 2]), 0.0) for q in qs]
        a_ak = [b16(jnp.where(strict, expand(s[q][:rows, QUAD // 2:]), 0.0)) for q in qs]
        a_rb = [b16(jnp.where(incl, expand(s[q][rows:, :QUAD // 2]), 0.0)) for q in qs]
        a_rk = [b16(jnp.where(incl, expand(s[q][rows:, QUAD // 2:]), 0.0)) for q in qs]

        a_d = [jnp.where(leaf, a_ab[q], 0.0) for q in qs]
        t_inv = [eye_f + a_d[q] for q in qs]
        pw = [b16(a_d[q]) for q in qs]
        for _ in range(3):
            pw = [b16(_mm(pw[q], pw[q])) for q in qs]
            t_inv = [t_inv[q] + _mm(pw[q], b16(t_inv[q])) for q in qs]
        for a_off in ([b16(jnp.where(in32 & jnp.logical_not(leaf), a_ab[q], 0.0)) for q in qs],
                      [b16(jnp.where(in32, 0.0, a_ab[q])) for q in qs]):
            t16 = [b16(t_inv[q]) for q in qs]
            ta = [b16(_mm(t16[q], a_off[q])) for q in qs]
            t_inv = [t_inv[q] + _mm(ta[q], t16[q]) for q in qs]

        akv = [b16(_mm(a_ak[q], v_s[q])) for q in qs]
        wu = [_mm(b16(t_inv[q]), jnp.concatenate([ah_s[q], akv[q]], axis=1)) for q in qs]
        wt = [b16(wu[q][:, :QUAD]) for q in qs]
        ut = [wu[q][:, QUAD:] for q in qs]
        bk = [jnp.concatenate([stack(bh[q] * pc[q]), stack(kh[q] * pc[q])], axis=0).T for q in qs]
        pc_col = [jnp.sum(jnp.where(eye, pc[q], 0.0), axis=1, keepdims=True) for q in qs]
        return dict(rh_s=rh_s, v_s=v_s, a_rb=a_rb, a_rk=a_rk, wt=wt, ut=ut, bk=bk, pc_col=pc_col)

    def wkv_state(w):
        rh_s, v_s, a_rb, a_rk = w["rh_s"], w["v_s"], w["a_rb"], w["a_rk"]
        wt, ut, bk, pc_col = w["wt"], w["ut"], w["bk"], w["pc_col"]
        m_st = [state_ref[q] for q in qs]
        m16 = [b16(m_st[q]) for q in qs]
        u16 = [b16(_mm(wt[q], m16[q]) + ut[q]) for q in qs]
        muv = [jnp.concatenate([m16[q], u16[q], v_s[q]], axis=0) for q in qs]
        y_s = [_mm(jnp.concatenate([rh_s[q], a_rb[q], a_rk[q]], axis=1), muv[q]) for q in qs]
        ys = []
        for q in qs:
            y = y_s[q][0:c_sz]
            for h in range(1, HEADS_PER_QUAD):
                y = y + y_s[q][h * c_sz:(h + 1) * c_sz]
            ys.append(y)
            state_ref[q] = (pc_col[q] * m_st[q]
                            + _mm(bk[q], jnp.concatenate([u16[q], v_s[q]], axis=0)))
        return ys

    def post_chunk(c, ys, d):
        y = jnp.concatenate(ys, axis=0)
        yc = y - _mm(y, avg_bd)
        var = _mm(yc * yc, avg_bd)
        yn = yc * lax.rsqrt(var + LNX_EPS)
        yn = jnp.concatenate([yn[q * c_sz:(q + 1) * c_sz] for q in qs], axis=1)
        out = (yn * lng_ref[...] + lnb_ref[...] + d["bonus"]) * d["g"]
        o_ref[c * c_sz:(c + 1) * c_sz, :] = out.astype(o_ref.dtype)

    n_chunks = tb // c_sz
    nxt = prep_chunk(0)
    nxt_w = wkv_chunk(nxt)
    for c in range(n_chunks):
        cur, cur_w = nxt, nxt_w
        if c + 1 < n_chunks:
            nxt = prep_chunk(c + 1)
            nxt_w = wkv_chunk(nxt)
        post_chunk(c, wkv_state(cur_w), cur)
    carry_ref[0:1, :] = p_ref[tb - 1:tb, :]


def _lane_window(start, stop):
    return (start // LANE) * LANE, -(-stop // LANE) * LANE


def _pad_rows_to_window(w, start, window):
    return jnp.pad(w, ((start - window[0], window[1] - start - w.shape[0]), (0, 0)))


def _rwkv_mix(p, mu, w0, w_lora_up, a0, a_lora_up, g_lora_up, k_k, k_a, r_k, lnx_g, lnx_b, cast, *,
              tb):
    bsz, t, cols = p.shape
    rw = w0.shape[1]
    o1 = 3 * rw
    o2 = o1 + w_lora_up.shape[0]
    o3 = o2 + a_lora_up.shape[0]
    o4 = o3 + g_lora_up.shape[0]
    win_w, win_a, win_g = _lane_window(o1, o2), _lane_window(o2, o3), _lane_window(o3, o4)
    assert win_g[1] == cols
    wup = _pad_rows_to_window(w_lora_up, o1, win_w)
    aup = _pad_rows_to_window(a_lora_up, o2, win_a)
    gup = _pad_rows_to_window(g_lora_up, o3, win_g)
    row_spec = lambda w: pl.BlockSpec((1, w), lambda b, i: (0, 0))
    full = lambda a: pl.BlockSpec(a.shape, lambda b, i: (0, 0))
    nt = t // tb
    steps = bsz * nt
    slabs = [_slab_specs(*c, steps, lambda b, i: b * nt + i) for c in cast]
    outs = pl.pallas_call(
        functools.partial(_rwkv_mix_kernel, rw=rw, win_w=win_w, win_a=win_a, win_g=win_g,
                          n_cast=len(cast)),
        grid=(bsz, nt),
        in_specs=[
            pl.BlockSpec((None, tb, cols), lambda b, i: (b, i, 0)),
            row_spec(cols), row_spec(rw), full(wup), row_spec(rw), full(aup), full(gup),
            row_spec(rw), row_spec(rw), row_spec(rw), row_spec(rw), row_spec(rw),
        ] + [sl[0] for sl in slabs],
        out_specs=[pl.BlockSpec((None, tb, rw), lambda b, i: (b, i, 0))] + [sl[1] for sl in slabs],
        out_shape=[jax.ShapeDtypeStruct((bsz, t, rw), BF16)] + [sl[2] for sl in slabs],
        scratch_shapes=[pltpu.VMEM((SUBLANE, cols), F32),
                        pltpu.VMEM((rw // QUAD, QUAD, QUAD), F32)],
        compiler_params=pltpu.CompilerParams(
            dimension_semantics=("parallel", "arbitrary"), vmem_limit_bytes=VMEM_LIMIT),
        name="rwkv_mix",
    )(p, mu, w0, wup, a0, aup, gup, k_k, k_a, r_k, lnx_g, lnx_b, *[a for a, _, _ in cast])
    return outs[0], outs[1:]


def _gelu(y):
    return 0.5 * y * (1.0 + lax.erf(y * (2.0 ** -0.5)))


def _front_kernel(x_ref, g_ref, wzt_ref, lng_ref, lnb_ref, ws_ref, bs_ref, cast_ref, n_ref, o_ref,
                  cast16_ref, w16_ref, *, sub):
    tm, width = o_ref.shape
    cast16_ref[...] = cast_ref[...].astype(BF16)

    @pl.when(pl.program_id(0) == 0)
    def _():
        w16_ref[...] = wzt_ref[...].astype(BF16)

    n = _rmsnorm(x_ref[...], g_ref[...]).astype(BF16)
    n_ref[...] = n

    def z_cols(c0):
        return _gelu(_mm_nt(n, w16_ref[c0:c0 + sub, :]))

    v = jnp.concatenate([z_cols(width + k * sub) for k in range(width // sub)], axis=1)
    mean = jnp.mean(v, axis=-1, keepdims=True)
    vc = v - mean
    var = jnp.mean(vc * vc, axis=-1, keepdims=True)
    vn = vc * lax.rsqrt(var + LN_EPS) * lng_ref[...] + lnb_ref[...]

    tr = lax.broadcasted_iota(jnp.int32, (SGU_CHUNK, SGU_CHUNK), 0)
    tc = lax.broadcasted_iota(jnp.int32, (SGU_CHUNK, SGU_CHUNK), 1)
    causal = tc <= tr
    for k in range(width // sub):
        u = z_cols(k * sub)
        for gi in range(sub // SGU_GROUP):
            g = k * (sub // SGU_GROUP) + gi
            w_c = jnp.where(causal, ws_ref[g], 0.0)
            lanes = slice(g * SGU_GROUP, (g + 1) * SGU_GROUP)
            ul = slice(gi * SGU_GROUP, (gi + 1) * SGU_GROUP)
            for c in range(tm // SGU_CHUNK):
                rows = slice(c * SGU_CHUNK, (c + 1) * SGU_CHUNK)
                mixed = _mm(w_c, vn[rows, lanes]) + bs_ref[:, lanes]
                o_ref[rows, lanes] = (u[rows, ul] * mixed).astype(o_ref.dtype)


def _front(x, g, w_t, row0, ln_g, ln_b, w_s, b_full, cast, *, tm, sub=256):
    m, d = x.shape
    width = ln_g.shape[1]
    cast_in, cast_out, cast_shape = _slab_specs(*cast, m // tm, lambda i: i)
    return pl.pallas_call(
        functools.partial(_front_kernel, sub=sub),
        grid=(m // tm,),
        in_specs=[
            pl.BlockSpec((tm, d), lambda i: (i, 0)),
            pl.BlockSpec((1, d), lambda i: (0, 0)),
            pl.BlockSpec((pl.Element(2 * width), pl.Element(d)), lambda i: (row0, 0),
                         pipeline_mode=pl.Buffered(1)),
            pl.BlockSpec((1, width), lambda i: (0, 0)),
            pl.BlockSpec((1, width), lambda i: (0, 0)),
            pl.BlockSpec(w_s.shape, lambda i: (0, 0, 0)),
            pl.BlockSpec(b_full.shape, lambda i: (0, 0)),
            cast_in,
        ],
        out_specs=[pl.BlockSpec((tm, d), lambda i: (i, 0)), pl.BlockSpec((tm, width), lambda i: (i, 0)),
                   cast_out],
        out_shape=[jax.ShapeDtypeStruct((m, d), BF16), jax.ShapeDtypeStruct((m, width), BF16),
                   cast_shape],
        scratch_shapes=[pltpu.VMEM((2 * width, d), BF16)],
        compiler_params=pltpu.CompilerParams(
            dimension_semantics=("arbitrary",), vmem_limit_bytes=VMEM_LIMIT),
        name="front",
    )(x, g, w_t, ln_g, ln_b, w_s, b_full, cast[0])


def _merge_kernel(n1_ref, ya_ref, yb_ref, x_ref, wgt_ref, wa_ref, wb_ref, wo_ref, g2_ref, h_ref,
                  n_ref, *, sub):
    d = x_ref.shape[1]
    n1 = n1_ref[...]
    ya = ya_ref[...]
    yb = yb_ref[...]
    parts = []
    for k in range(d // sub):
        cols = slice(k * sub, (k + 1) * sub)
        ga = jax.nn.sigmoid(_mm_nt(n1, wgt_ref[k * sub:(k + 1) * sub, :]))
        gb = jax.nn.sigmoid(_mm_nt(n1, wgt_ref[d + k * sub:d + (k + 1) * sub, :]))
        parts.append((ga * _mm(ya, wa_ref[:, cols]) + gb * _mm(yb, wb_ref[:, cols])).astype(BF16))
    h = x_ref[...] + _mm(jnp.concatenate(parts, axis=1), wo_ref[...])
    h_ref[...] = h
    n_ref[...] = _rmsnorm(h, g2_ref[...]).astype(BF16)


def _merge(n1, ya, yb, x, wgt, wa, wb, wo, g2, *, tm, sub=256):
    m, d = x.shape
    rw = ya.shape[1]
    const = lambda a: pl.BlockSpec(a.shape, lambda i: (0, 0), pipeline_mode=pl.Buffered(1))
    return pl.pallas_call(
        functools.partial(_merge_kernel, sub=sub),
        grid=(m // tm,),
        in_specs=[
            pl.BlockSpec((tm, d), lambda i: (i, 0)),
            pl.BlockSpec((tm, rw), lambda i: (i, 0)),
            pl.BlockSpec((tm, rw), lambda i: (i, 0)),
            pl.BlockSpec((tm, d), lambda i: (i, 0)),
            const(wgt), const(wa), const(wb), const(wo),
            pl.BlockSpec((1, d), lambda i: (0, 0)),
        ],
        out_specs=[pl.BlockSpec((tm, d), lambda i: (i, 0)), pl.BlockSpec((tm, d), lambda i: (i, 0))],
        out_shape=[jax.ShapeDtypeStruct((m, d), F32), jax.ShapeDtypeStruct((m, d), BF16)],
        compiler_params=pltpu.CompilerParams(
            dimension_semantics=("parallel",), vmem_limit_bytes=VMEM_LIMIT),
        name="merge",
    )(n1, ya, yb, x, wgt, wa, wb, wo, g2)


def _ffn_kernel(n_ref, h_ref, wg_ref, wu_ref, wd_ref, gf_ref, o_ref, *, sub):
    j = pl.program_id(1)

    @pl.when(j == 0)
    def _():
        o_ref[...] = h_ref[...]

    n = n_ref[...]
    part = None
    for k in range(wg_ref.shape[1] // sub):
        cols = slice(k * sub, (k + 1) * sub)
        gate = _mm(n, wg_ref[:, cols])
        up = _mm(n, wu_ref[:, cols])
        act = (gate * jax.nn.sigmoid(gate) * up).astype(BF16)
        down = _mm(act, wd_ref[cols, :])
        part = down if part is None else part + down
    o_ref[...] += part

    @pl.when(j == pl.num_programs(1) - 1)
    def _():
        o_ref[...] = _rmsnorm(o_ref[...], gf_ref[...])


def _ffn(n2, h1, wg, wu, wd, gf, *, tm, th, sub=256):
    m, d = h1.shape
    hid = wg.shape[1]
    return pl.pallas_call(
        functools.partial(_ffn_kernel, sub=sub),
        grid=(m // tm, hid // th),
        in_specs=[
            pl.BlockSpec((tm, d), lambda i, j: (i, 0)),
            pl.BlockSpec((tm, d), lambda i, j: (i, 0)),
            pl.BlockSpec((d, th), lambda i, j: (0, j)),
            pl.BlockSpec((d, th), lambda i, j: (0, j)),
            pl.BlockSpec((th, d), lambda i, j: (j, 0)),
            pl.BlockSpec((1, d), lambda i, j: (0, 0)),
        ],
        out_specs=pl.BlockSpec((tm, d), lambda i, j: (i, 0)),
        out_shape=jax.ShapeDtypeStruct((m, d), F32),
        compiler_params=pltpu.CompilerParams(
            dimension_semantics=("parallel", "arbitrary"), vmem_limit_bytes=VMEM_LIMIT),
        name="ffn",
    )(n2, h1, wg, wu, wd, gf)


def _pad_cols(a, width):
    return jnp.pad(a, ((0, 0), (0, width - a.shape[1])))


def _block(h, norm_mix_g, w_in, shift_mu, w0, w_lora_up, a0, a_lora_up, g_lora_up, k_k, k_a, r_k,
           lnx_g, lnx_b, w_proj_rwkv, sgu_ln_g, sgu_ln_b, sgu_w, sgu_b, w_proj_sgu, w_out,
           norm_ffn_g, w_ffn_gate, w_ffn_up, w_ffn_down, g_last, *, bsz, seq):
    m, d = h.shape
    rw = w0.shape[0]
    sw = sgu_ln_g.shape[0]
    rcols = shift_mu.shape[0]
    zc = rcols + 2 * sw
    rcols_al = -(-rcols // LANE) * LANE
    row = lambda a: a.reshape(1, -1)

    w_t = w_in.T
    mu = _pad_cols(shift_mu[None, :], rcols_al)

    b_full = jnp.repeat(sgu_b.T, SGU_GROUP, axis=1)
    n1, yb, w_rwkv16t = _front(h, row(norm_mix_g), w_t, rcols, row(sgu_ln_g), row(sgu_ln_b), sgu_w,
                               b_full, (w_t, 0, rcols_al), tm=512)
    p = _proj(n1, w_rwkv16t, tm=512)

    whole = lambda a: (a, 0, a.shape[0])
    ya, (wgt, wa, wb, wo, wg, wu, wd) = _rwkv_mix(
        p.reshape(bsz, seq, -1), mu, row(w0), w_lora_up, row(a0), a_lora_up, g_lora_up,
        row(k_k), row(k_a), row(r_k), row(lnx_g), row(lnx_b),
        ((w_t, zc, 2 * d), whole(w_proj_rwkv), whole(w_proj_sgu), whole(w_out),
         whole(w_ffn_gate), whole(w_ffn_up), whole(w_ffn_down)), tb=256)

    h1, n2 = _merge(n1, ya.reshape(m, rw), yb, h, wgt, wa, wb, wo, row(norm_ffn_g), tm=256)
    return _ffn(n2, h1, wg, wu, wd, row(g_last), tm=1024, th=512)


def kernel(x, norm_mix_g, w_in, shift_mu, w0, w_lora_up, a0, a_lora_up, g_lora_up, k_k, k_a, r_k,
           lnx_g, lnx_b, w_proj_rwkv, sgu_ln_g, sgu_ln_b, sgu_w, sgu_b, w_proj_sgu, w_out,
           norm_ffn_g, w_ffn_gate, w_ffn_up, w_ffn_down, norm_final_g):
    bsz, seq, d = x.shape
    depth = w_in.shape[0]
    assert depth == 1, "the final rmsnorm is fused into the (single) layer's channel mixer"
    h = x.reshape(bsz * seq, d)
    out = _block(h, norm_mix_g[0], w_in[0], shift_mu[0], w0[0], w_lora_up[0], a0[0], a_lora_up[0],
                 g_lora_up[0], k_k[0], k_a[0], r_k[0], lnx_g[0], lnx_b[0], w_proj_rwkv[0],
                 sgu_ln_g[0], sgu_ln_b[0], sgu_w[0], sgu_b[0], w_proj_sgu[0], w_out[0],
                 norm_ffn_g[0], w_ffn_gate[0], w_ffn_up[0], w_ffn_down[0], norm_final_g,
                 bsz=bsz, seq=seq)
    return out.reshape(bsz, seq, d)
```

```python
import functools

import jax
import jax.numpy as jnp
from jax import lax
from jax.experimental import pallas as pl
from jax.experimental.pallas import tpu as pltpu

F32 = jnp.float32
BF16 = jnp.bfloat16

RMS_EPS = 1e-6
LN_EPS = 1e-5
LNX_EPS = 64e-5

HEAD = 64
WKV_CHUNK = 64
QUAD = 256
HEADS_PER_QUAD = QUAD // HEAD
SGU_CHUNK = 128
SGU_GROUP = 128
LANE = 128
SUBLANE = 8

VMEM_LIMIT = 60 * 1024 * 1024


def _mm(a, b):
    return jnp.dot(a, b, preferred_element_type=F32)


def _mm_nt(a, b):
    return lax.dot_general(a, b, (((1,), (1,)), ((), ())), preferred_element_type=F32)


def _rmsnorm(x, g):
    return x * lax.rsqrt(jnp.mean(x * x, axis=-1, keepdims=True) + RMS_EPS) * g


def _proj_kernel(n_ref, w16t_ref, o_ref, *, sub):
    for k in range(o_ref.shape[1] // sub):
        cols = slice(k * sub, (k + 1) * sub)
        o_ref[:, cols] = _mm_nt(n_ref[...], w16t_ref[cols, :])


def _proj(n, w16t, *, tm, sub=256):
    m, d = n.shape
    n_cols = w16t.shape[0]
    return pl.pallas_call(
        functools.partial(_proj_kernel, sub=sub),
        grid=(m // tm,),
        in_specs=[
            pl.BlockSpec((tm, d), lambda i: (i, 0)),
            pl.BlockSpec((n_cols, d), lambda i: (0, 0), pipeline_mode=pl.Buffered(1)),
        ],
        out_specs=pl.BlockSpec((tm, n_cols), lambda i: (i, 0)),
        out_shape=jax.ShapeDtypeStruct((m, n_cols), F32),
        compiler_params=pltpu.CompilerParams(
            dimension_semantics=("parallel",), vmem_limit_bytes=VMEM_LIMIT),
        name="proj",
    )(n, w16t)


def _slab_specs(a, row0, nrows, steps, step_index):
    assert row0 % SUBLANE == 0 and nrows % (steps * 2 * SUBLANE) == 0
    r = nrows // steps
    cols = a.shape[1]
    in_spec = pl.BlockSpec(
        (pl.Element(r), pl.Element(cols)),
        lambda *g: (pl.multiple_of(row0 + step_index(*g) * r, SUBLANE), 0))
    out_spec = pl.BlockSpec((r, cols), lambda *g: (step_index(*g), 0))
    return in_spec, out_spec, jax.ShapeDtypeStruct((nrows, cols), BF16)


def _rwkv_mix_kernel(p_ref, mu_ref, w0_ref, wup_ref, a0_ref, aup_ref, gup_ref, kk_ref, ka_ref,
                     rk_ref, lng_ref, lnb_ref, *rest, rw, win_w, win_a, win_g, n_cast):
    cast_in, o_ref, cast_out = rest[:n_cast], rest[n_cast], rest[n_cast + 1:2 * n_cast + 1]
    carry_ref, state_ref = rest[2 * n_cast + 1:]
    for src_ref, dst_ref in zip(cast_in, cast_out):
        dst_ref[...] = src_ref[...].astype(dst_ref.dtype)

    tb = p_ref.shape[0]
    c_sz = WKV_CHUNK
    n_quads = rw // QUAD
    qs = range(n_quads)
    lanes = [slice(q * QUAD, (q + 1) * QUAD) for q in qs]
    rows = HEADS_PER_QUAD * c_sz
    assert rows == QUAD

    @pl.when(pl.program_id(1) == 0)
    def _():
        carry_ref[...] = jnp.zeros_like(carry_ref)
        state_ref[...] = jnp.zeros_like(state_ref)

    ri = lax.broadcasted_iota(jnp.int32, (rows, QUAD), 0)
    ci = lax.broadcasted_iota(jnp.int32, (rows, QUAD), 1)
    ones_bd = jnp.where((ri // HEAD) == (ci // HEAD), 1.0, 0.0).astype(F32)
    avg_bd = ones_bd * (1.0 / HEAD)
    head_mask = (ri // c_sz) == (ci // HEAD)
    same = (ri // c_sz) == (ci // c_sz)
    strict = same & (ci < ri)
    incl = same & (ci <= ri)
    leaf = (ri // 16) == (ci // 16)
    in32 = (ri // 32) == (ci // 32)
    eye = ri == ci
    eye_f = jnp.where(eye, 1.0, 0.0).astype(F32)
    crow = lax.broadcasted_iota(jnp.int32, (c_sz, 1), 0)

    def head_sums(xs):
        lhs = jnp.concatenate([x[:, lanes[q]] for x in xs for q in qs], axis=0)
        out = _mm(lhs, ones_bd)
        return [jnp.concatenate([out[(i * n_quads + q) * c_sz:(i * n_quads + q + 1) * c_sz]
                                 for q in qs], axis=1) for i in range(len(xs))]

    def b16(x):
        return x.astype(BF16)

    def stack(x):
        return jnp.where(head_mask, jnp.concatenate([b16(x)] * HEADS_PER_QUAD, axis=0), 0.0)

    def expand(x):
        return jnp.concatenate([x, x], axis=1)

    def prep_chunk(c):
        r0 = c * c_sz
        p = p_ref[r0:r0 + c_sz, :]
        before = carry_ref[0:1, :] if c == 0 else p_ref[r0 - 1:r0, :]
        prev = jnp.where(crow == 0, before, pltpu.roll(p, 1, 0))
        sh = p + (prev - p) * mu_ref[...]
        r = sh[:, 0:rw]
        k = sh[:, rw:2 * rw]
        v = sh[:, 2 * rw:3 * rw]
        xw = sh[:, win_w[0]:win_w[1]]
        xa = sh[:, win_a[0]:win_a[1]]
        xg = sh[:, win_g[0]:win_g[1]]

        wl = w0_ref[...] + _mm(jnp.tanh(xw), wup_ref[...])
        lw = (-(2.718281828459045 ** -0.5)) * jax.nn.sigmoid(wl)
        a = jax.nn.sigmoid(a0_ref[...] + _mm(xa, aup_ref[...]))
        g = _mm(jax.nn.sigmoid(xg), gup_ref[...])

        kk = k * kk_ref[...]
        k2 = k * (1.0 + (a - 1.0) * ka_ref[...])
        kk_ss, rk_sum = head_sums([kk * kk, r * k2 * rk_ref[...]])
        kk = kk * jnp.minimum(lax.rsqrt(kk_ss), 1e12)
        bonus = rk_sum * v

        cum = lw
        shift = 1
        while shift < c_sz:
            cum = cum + jnp.where(crow >= shift, pltpu.roll(cum, shift, 0), 0.0)
            shift *= 2
        ec = jnp.exp(cum)
        eci = jnp.exp(-cum)
        ec_prev = jnp.where(crow == 0, 1.0, pltpu.roll(ec, 1, 0))
        return dict(ah=-kk * ec_prev, rh=r * ec, bh=kk * a * eci, kh=k2 * eci, v=v,
                    pc=ec[c_sz - 1:c_sz, :], bonus=bonus, g=g)

    def wkv_chunk(d):
        ah = [d["ah"][:, lanes[q]] for q in qs]
        rh = [d["rh"][:, lanes[q]] for q in qs]
        bh = [d["bh"][:, lanes[q]] for q in qs]
        kh = [d["kh"][:, lanes[q]] for q in qs]
        vv = [d["v"][:, lanes[q]] for q in qs]
        pc = [d["pc"][:, lanes[q]] for q in qs]

        ah_s = [stack(x) for x in ah]
        rh_s = [stack(x) for x in rh]
        v_s = [stack(x) for x in vv]
        s = [_mm_nt(jnp.concatenate([ah_s[q], rh_s[q]], axis=0),
                    b16(jnp.concatenate([bh[q], bh[q], kh[q], kh[q]], axis=0))) for q in qs]
        a_ab = [jnp.where(strict, expand(s[q][:rows, :QUAD // 2]), 0.0) for q in qs]
        a_ak = [b16(jnp.where(strict, expand(s[q][:rows, QUAD // 2:]), 0.0)) for q in qs]
        a_rb = [b16(jnp.where(incl, expand(s[q][rows:, :QUAD // 2]), 0.0)) for q in qs]
        a_rk = [b16(jnp.where(incl, expand(s[q][rows:, QUAD // 2:]), 0.0)) for q in qs]

        a_d = [jnp.where(leaf, a_ab[q], 0.0) for q in qs]
        t_inv = [eye_f + a_d[q] for q in qs]
        pw = [b16(a_d[q]) for q in qs]
        for _ in range(3):
            pw = [b16(_mm(pw[q], pw[q])) for q in qs]
            t_inv = [t_inv[q] + _mm(pw[q], b16(t_inv[q])) for q in qs]
        for a_off in ([b16(jnp.where(in32 & jnp.logical_not(leaf), a_ab[q], 0.0)) for q in qs],
                      [b16(jnp.where(in32, 0.0, a_ab[q])) for q in qs]):
            t16 = [b16(t_inv[q]) for q in qs]
            ta = [b16(_mm(t16[q], a_off[q])) for q in qs]
            t_inv = [t_inv[q] + _mm(ta[q], t16[q]) for q in qs]

        akv = [b16(_mm(a_ak[q], v_s[q])) for q in qs]
        wu = [_mm(b16(t_inv[q]), jnp.concatenate([ah_s[q], akv[q]], axis=1)) for q in qs]
        wt = [b16(wu[q][:, :QUAD]) for q in qs]
        ut = [wu[q][:, QUAD:] for q in qs]
        bk = [jnp.concatenate([stack(bh[q] * pc[q]), stack(kh[q] * pc[q])], axis=0).T for q in qs]
        pc_col = [jnp.sum(jnp.where(eye, pc[q], 0.0), axis=1, keepdims=True) for q in qs]
        return dict(rh_s=rh_s, v_s=v_s, a_rb=a_rb, a_rk=a_rk, wt=wt, ut=ut, bk=bk, pc_col=pc_col)

    def wkv_state(w):
        rh_s, v_s, a_rb, a_rk = w["rh_s"], w["v_s"], w["a_rb"], w["a_rk"]
        wt, ut, bk, pc_col = w["wt"], w["ut"], w["bk"], w["pc_col"]
        m_st = [state_ref[q] for q in qs]
        m16 = [b16(m_st[q]) for q in qs]
        u16 = [b16(_mm(wt[q], m16[q]) + ut[q]) for q in qs]
        muv = [jnp.concatenate([m16[q], u16[q], v_s[q]], axis=0) for q in qs]
        y_s = [_mm(jnp.concatenate([rh_s[q], a_rb[q], a_rk[q]], axis=1), muv[q]) for q in qs]
        ys = []
        for q in qs:
            y = y_s[q][0:c_sz]
            for h in range(1, HEADS_PER_QUAD):
                y = y + y_s[q][h * c_sz:(h + 1) * c_sz]
            ys.append(y)
            state_ref[q] = (pc_col[q] * m_st[q]
                            + _mm(bk[q], jnp.concatenate([u16[q], v_s[q]], axis=0)))
        return ys

    def post_chunk(c, ys, d):
        y = jnp.concatenate(ys, axis=0)
        yc = y - _mm(y, avg_bd)
        var = _mm(yc * yc, avg_bd)
        yn = yc * lax.rsqrt(var + LNX_EPS)
        yn = jnp.concatenate([yn[q * c_sz:(q + 1) * c_sz] for q in qs], axis=1)
        out = (yn * lng_ref[...] + lnb_ref[...] + d["bonus"]) * d["g"]
        o_ref[c * c_sz:(c + 1) * c_sz, :] = out.astype(o_ref.dtype)

    n_chunks = tb // c_sz
    nxt = prep_chunk(0)
    nxt_w = wkv_chunk(nxt)
    for c in range(n_chunks):
        cur, cur_w = nxt, nxt_w
        if c + 1 < n_chunks:
            nxt = prep_chunk(c + 1)
            nxt_w = wkv_chunk(nxt)
        post_chunk(c, wkv_state(cur_w), cur)
    carry_ref[0:1, :] = p_ref[tb - 1:tb, :]


def _lane_window(start, stop):
    return (start // LANE) * LANE, -(-stop // LANE) * LANE


def _pad_rows_to_window(w, start, window):
    return jnp.pad(w, ((start - window[0], window[1] - start - w.shape[0]), (0, 0)))


def _rwkv_mix(p, mu, w0, w_lora_up, a0, a_lora_up, g_lora_up, k_k, k_a, r_k, lnx_g, lnx_b, cast, *,
              tb):
    bsz, t, cols = p.shape
    rw = w0.shape[1]
    o1 = 3 * rw
    o2 = o1 + w_lora_up.shape[0]
    o3 = o2 + a_lora_up.shape[0]
    o4 = o3 + g_lora_up.shape[0]
    win_w, win_a, win_g = _lane_window(o1, o2), _lane_window(o2, o3), _lane_window(o3, o4)
    assert win_g[1] == cols
    wup = _pad_rows_to_window(w_lora_up, o1, win_w)
    aup = _pad_rows_to_window(a_lora_up, o2, win_a)
    gup = _pad_rows_to_window(g_lora_up, o3, win_g)
    row_spec = lambda w: pl.BlockSpec((1, w), lambda b, i: (0, 0))
    full = lambda a: pl.BlockSpec(a.shape, lambda b, i: (0, 0))
    nt = t // tb
    steps = bsz * nt
    slabs = [_slab_specs(*c, steps, lambda b, i: b * nt + i) for c in cast]
    outs = pl.pallas_call(
        functools.partial(_rwkv_mix_kernel, rw=rw, win_w=win_w, win_a=win_a, win_g=win_g,
                          n_cast=len(cast)),
        grid=(bsz, nt),
        in_specs=[
            pl.BlockSpec((None, tb, cols), lambda b, i: (b, i, 0)),
            row_spec(cols), row_spec(rw), full(wup), row_spec(rw), full(aup), full(gup),
            row_spec(rw), row_spec(rw), row_spec(rw), row_spec(rw), row_spec(rw),
        ] + [sl[0] for sl in slabs],
        out_specs=[pl.BlockSpec((None, tb, rw), lambda b, i: (b, i, 0))] + [sl[1] for sl in slabs],
        out_shape=[jax.ShapeDtypeStruct((bsz, t, rw), BF16)] + [sl[2] for sl in slabs],
        scratch_shapes=[pltpu.VMEM((SUBLANE, cols), F32),
                        pltpu.VMEM((rw // QUAD, QUAD, QUAD), F32)],
        compiler_params=pltpu.CompilerParams(
            dimension_semantics=("parallel", "arbitrary"), vmem_limit_bytes=VMEM_LIMIT),
        name="rwkv_mix",
    )(p, mu, w0, wup, a0, aup, gup, k_k, k_a, r_k, lnx_g, lnx_b, *[a for a, _, _ in cast])
    return outs[0], outs[1:]


def _gelu(y):
    return 0.5 * y * (1.0 + lax.erf(y * (2.0 ** -0.5)))


def _front_kernel(x_ref, g_ref, wzt_ref, lng_ref, lnb_ref, ws_ref, bs_ref, cast_ref, n_ref, o_ref,
                  cast16_ref, w16_ref, *, sub):
    tm, width = o_ref.shape
    cast16_ref[...] = cast_ref[...].astype(BF16)

    @pl.when(pl.program_id(0) == 0)
    def _():
        w16_ref[...] = wzt_ref[...].astype(BF16)

    n = _rmsnorm(x_ref[...], g_ref[...]).astype(BF16)
    n_ref[...] = n

    def z_cols(c0):
        return _gelu(_mm_nt(n, w16_ref[c0:c0 + sub, :]))

    v = jnp.concatenate([z_cols(width + k * sub) for k in range(width // sub)], axis=1)
    mean = jnp.mean(v, axis=-1, keepdims=True)
    vc = v - mean
    var = jnp.mean(vc * vc, axis=-1, keepdims=True)
    vn = vc * lax.rsqrt(var + LN_EPS) * lng_ref[...] + lnb_ref[...]

    tr = lax.broadcasted_iota(jnp.int32, (SGU_CHUNK, SGU_CHUNK), 0)
    tc = lax.broadcasted_iota(jnp.int32, (SGU_CHUNK, SGU_CHUNK), 1)
    causal = tc <= tr
    for k in range(width // sub):
        u = z_cols(k * sub)
        for gi in range(sub // SGU_GROUP):
            g = k * (sub // SGU_GROUP) + gi
            w_c = jnp.where(causal, ws_ref[g], 0.0)
            lanes = slice(g * SGU_GROUP, (g + 1) * SGU_GROUP)
            ul = slice(gi * SGU_GROUP, (gi + 1) * SGU_GROUP)
            for c in range(tm // SGU_CHUNK):
                rows = slice(c * SGU_CHUNK, (c + 1) * SGU_CHUNK)
                mixed = _mm(w_c, vn[rows, lanes]) + bs_ref[:, lanes]
                o_ref[rows, lanes] = (u[rows, ul] * mixed).astype(o_ref.dtype)


def _front(x, g, w_t, row0, ln_g, ln_b, w_s, b_full, cast, *, tm, sub=256):
    m, d = x.shape
    width = ln_g.shape[1]
    cast_in, cast_out, cast_shape = _slab_specs(*cast, m // tm, lambda i: i)
    return pl.pallas_call(
        functools.partial(_front_kernel, sub=sub),
        grid=(m // tm,),
        in_specs=[
            pl.BlockSpec((tm, d), lambda i: (i, 0)),
            pl.BlockSpec((1, d), lambda i: (0, 0)),
            pl.BlockSpec((pl.Element(2 * width), pl.Element(d)), lambda i: (row0, 0),
                         pipeline_mode=pl.Buffered(1)),
            pl.BlockSpec((1, width), lambda i: (0, 0)),
            pl.BlockSpec((1, width), lambda i: (0, 0)),
            pl.BlockSpec(w_s.shape, lambda i: (0, 0, 0)),
            pl.BlockSpec(b_full.shape, lambda i: (0, 0)),
            cast_in,
        ],
        out_specs=[pl.BlockSpec((tm, d), lambda i: (i, 0)), pl.BlockSpec((tm, width), lambda i: (i, 0)),
                   cast_out],
        out_shape=[jax.ShapeDtypeStruct((m, d), BF16), jax.ShapeDtypeStruct((m, width), BF16),
                   cast_shape],
        scratch_shapes=[pltpu.VMEM((2 * width, d), BF16)],
        compiler_params=pltpu.CompilerParams(
            dimension_semantics=("arbitrary",), vmem_limit_bytes=VMEM_LIMIT),
        name="front",
    )(x, g, w_t, ln_g, ln_b, w_s, b_full, cast[0])


def _merge_kernel(n1_ref, ya_ref, yb_ref, x_ref, wgt_ref, wa_ref, wb_ref, wo_ref, g2_ref, h_ref,
                  n_ref, *, sub):
    d = x_ref.shape[1]
    n1 = n1_ref[...]
    ya = ya_ref[...]
    yb = yb_ref[...]
    parts = []
    for k in range(d // sub):
        cols = slice(k * sub, (k + 1) * sub)
        ga = jax.nn.sigmoid(_mm_nt(n1, wgt_ref[k * sub:(k + 1) * sub, :]))
        gb = jax.nn.sigmoid(_mm_nt(n1, wgt_ref[d + k * sub:d + (k + 1) * sub, :]))
        parts.append((ga * _mm(ya, wa_ref[:, cols]) + gb * _mm(yb, wb_ref[:, cols])).astype(BF16))
    h = x_ref[...] + _mm(jnp.concatenate(parts, axis=1), wo_ref[...])
    h_ref[...] = h
    n_ref[...] = _rmsnorm(h, g2_ref[...]).astype(BF16)


def _merge(n1, ya, yb, x, wgt, wa, wb, wo, g2, *, tm, sub=256):
    m, d = x.shape
    rw = ya.shape[1]
    const = lambda a: pl.BlockSpec(a.shape, lambda i: (0, 0), pipeline_mode=pl.Buffered(1))
    return pl.pallas_call(
        functools.partial(_merge_kernel, sub=sub),
        grid=(m // tm,),
        in_specs=[
            pl.BlockSpec((tm, d), lambda i: (i, 0)),
            pl.BlockSpec((tm, rw), lambda i: (i, 0)),
            pl.BlockSpec((tm, rw), lambda i: (i, 0)),
            pl.BlockSpec((tm, d), lambda i: (i, 0)),
            const(wgt), const(wa), const(wb), const(wo),
            pl.BlockSpec((1, d), lambda i: (0, 0)),
        ],
        out_specs=[pl.BlockSpec((tm, d), lambda i: (i, 0)), pl.BlockSpec((tm, d), lambda i: (i, 0))],
        out_shape=[jax.ShapeDtypeStruct((m, d), F32), jax.ShapeDtypeStruct((m, d), BF16)],
        compiler_params=pltpu.CompilerParams(
            dimension_semantics=("parallel",), vmem_limit_bytes=VMEM_LIMIT),
        name="merge",
    )(n1, ya, yb, x, wgt, wa, wb, wo, g2)


def _ffn_kernel(n_ref, h_ref, wg_ref, wu_ref, wd_ref, gf_ref, o_ref, *, sub):
    j = pl.program_id(1)

    @pl.when(j == 0)
    def _():
        o_ref[...] = h_ref[...]

    n = n_ref[...]
    part = None
    for k in range(wg_ref.shape[1] // sub):
        cols = slice(k * sub, (k + 1) * sub)
        gate = _mm(n, wg_ref[:, cols])
        up = _mm(n, wu_ref[:, cols])
        act = (gate * jax.nn.sigmoid(gate) * up).astype(BF16)
        down = _mm(act, wd_ref[cols, :])
        part = down if part is None else part + down
    o_ref[...] += part

    @pl.when(j == pl.num_programs(1) - 1)
    def _():
        o_ref[...] = _rmsnorm(o_ref[...], gf_ref[...])


def _ffn(n2, h1, wg, wu, wd, gf, *, tm, th, sub=256):
    m, d = h1.shape
    hid = wg.shape[1]
    return pl.pallas_call(
        functools.partial(_ffn_kernel, sub=sub),
        grid=(m // tm, hid // th),
        in_specs=[
            pl.BlockSpec((tm, d), lambda i, j: (i, 0)),
            pl.BlockSpec((tm, d), lambda i, j: (i, 0)),
            pl.BlockSpec((d, th), lambda i, j: (0, j)),
            pl.BlockSpec((d, th), lambda i, j: (0, j)),
            pl.BlockSpec((th, d), lambda i, j: (j, 0)),
            pl.BlockSpec((1, d), lambda i, j: (0, 0)),
        ],
        out_specs=pl.BlockSpec((tm, d), lambda i, j: (i, 0)),
        out_shape=jax.ShapeDtypeStruct((m, d), F32),
        compiler_params=pltpu.CompilerParams(
            dimension_semantics=("parallel", "arbitrary"), vmem_limit_bytes=VMEM_LIMIT),
        name="ffn",
    )(n2, h1, wg, wu, wd, gf)


def _pad_cols(a, width):
    return jnp.pad(a, ((0, 0), (0, width - a.shape[1])))


def _block(h, norm_mix_g, w_in, shift_mu, w0, w_lora_up, a0, a_lora_up, g_lora_up, k_k, k_a, r_k,
           lnx_g, lnx_b, w_proj_rwkv, sgu_ln_g, sgu_ln_b, sgu_w, sgu_b, w_proj_sgu, w_out,
           norm_ffn_g, w_ffn_gate, w_ffn_up, w_ffn_down, g_last, *, bsz, seq):
    m, d = h.shape
    rw = w0.shape[0]
    sw = sgu_ln_g.shape[0]
    rcols = shift_mu.shape[0]
    zc = rcols + 2 * sw
    rcols_al = -(-rcols // LANE) * LANE
    row = lambda a: a.reshape(1, -1)

    w_t = w_in.T
    mu = _pad_cols(shift_mu[None, :], rcols_al)

    b_full = jnp.repeat(sgu_b.T, SGU_GROUP, axis=1)
    n1, yb, w_rwkv16t = _front(h, row(norm_mix_g), w_t, rcols, row(sgu_ln_g), row(sgu_ln_b), sgu_w,
                               b_full, (w_t, 0, rcols_al), tm=512)
    p = _proj(n1, w_rwkv16t, tm=512)

    whole = lambda a: (a, 0, a.shape[0])
    ya, (wgt, wa, wb, wo, wg, wu, wd) = _rwkv_mix(
        p.reshape(bsz, seq, -1), mu, row(w0), w_lora_up, row(a0), a_lora_up, g_lora_up,
        row(k_k), row(k_a), row(r_k), row(lnx_g), row(lnx_b),
        ((w_t, zc, 2 * d), whole(w_proj_rwkv), whole(w_proj_sgu), whole(w_out),
         whole(w_ffn_gate), whole(w_ffn_up), whole(w_ffn_down)), tb=256)

    h1, n2 = _merge(n1, ya.reshape(m, rw), yb, h, wgt, wa, wb, wo, row(norm_ffn_g), tm=256)
    return _ffn(n2, h1, wg, wu, wd, row(g_last), tm=1024, th=512)


def kernel(x, norm_mix_g, w_in, shift_mu, w0, w_lora_up, a0, a_lora_up, g_lora_up, k_k, k_a, r_k,
           lnx_g, lnx_b, w_proj_rwkv, sgu_ln_g, sgu_ln_b, sgu_w, sgu_b, w_proj_sgu, w_out,
           norm_ffn_g, w_ffn_gate, w_ffn_up, w_ffn_down, norm_final_g):
    bsz, seq, d = x.shape
    depth = w_in.shape[0]
    assert depth == 1, "the final rmsnorm is fused into the (single) layer's channel mixer"
    h = x.reshape(bsz * seq, d)
    out = _block(h, norm_mix_g[0], w_in[0], shift_mu[0], w0[0], w_lora_up[0], a0[0], a_lora_up[0],
                 g_lora_up[0], k_k[0], k_a[0], r_k[0], lnx_g[0], lnx_b[0], w_proj_rwkv[0],
                 sgu_ln_g[0], sgu_ln_b[0], sgu_w[0], sgu_b[0], w_proj_sgu[0], w_out[0],
                 norm_ffn_g[0], w_ffn_gate[0], w_ffn_up[0], w_ffn_down[0], norm_final_g,
                 bsz=bsz, seq=seq)
    return out.reshape(bsz, seq, d)
```

```python
import functools

import jax
import jax.numpy as jnp
from jax import lax
from jax.experimental import pallas as pl
from jax.experimental.pallas import tpu as pltpu

F32 = jnp.float32
BF16 = jnp.bfloat16

RMS_EPS = 1e-6
LN_EPS = 1e-5
LNX_EPS = 64e-5

HEAD = 64
WKV_CHUNK = 64
INV_LEAF = 8
QUAD = 256
HEADS_PER_QUAD = QUAD // HEAD
SGU_CHUNK = 128
SGU_GROUP = 128
LANE = 128
SUBLANE = 8

VMEM_LIMIT = 60 * 1024 * 1024


def _mm(a, b):
    return jnp.dot(a, b, preferred_element_type=F32)


def _mm_nt(a, b):
    return lax.dot_general(a, b, (((1,), (1,)), ((), ())), preferred_element_type=F32)


def _rmsnorm(x, g):
    return x * lax.rsqrt(jnp.mean(x * x, axis=-1, keepdims=True) + RMS_EPS) * g


def _proj_kernel(n_ref, w16t_ref, o_ref, *, sub):
    for k in range(o_ref.shape[1] // sub):
        cols = slice(k * sub, (k + 1) * sub)
        o_ref[:, cols] = _mm_nt(n_ref[...], w16t_ref[cols, :])


def _proj(n, w16t, *, tm, sub=256):
    m, d = n.shape
    n_cols = w16t.shape[0]
    return pl.pallas_call(
        functools.partial(_proj_kernel, sub=sub),
        grid=(m // tm,),
        in_specs=[
            pl.BlockSpec((tm, d), lambda i: (i, 0)),
            pl.BlockSpec((n_cols, d), lambda i: (0, 0), pipeline_mode=pl.Buffered(1)),
        ],
        out_specs=pl.BlockSpec((tm, n_cols), lambda i: (i, 0)),
        out_shape=jax.ShapeDtypeStruct((m, n_cols), F32),
        compiler_params=pltpu.CompilerParams(
            dimension_semantics=("parallel",), vmem_limit_bytes=VMEM_LIMIT),
        name="proj",
    )(n, w16t)


def _slab_specs(a, row0, nrows, steps, step_index):
    assert row0 % SUBLANE == 0 and nrows % (steps * 2 * SUBLANE) == 0
    r = nrows // steps
    cols = a.shape[1]
    in_spec = pl.BlockSpec(
        (pl.Element(r), pl.Element(cols)),
        lambda *g: (pl.multiple_of(row0 + step_index(*g) * r, SUBLANE), 0))
    out_spec = pl.BlockSpec((r, cols), lambda *g: (step_index(*g), 0))
    return in_spec, out_spec, jax.ShapeDtypeStruct((nrows, cols), BF16)


def _rwkv_mix_kernel(p_ref, mu_ref, w0_ref, wup_ref, a0_ref, aup_ref, gup_ref, kk_ref, ka_ref,
                     rk_ref, lng_ref, lnb_ref, *rest, rw, win_w, win_a, win_g, n_cast):
    cast_in, o_ref, cast_out = rest[:n_cast], rest[n_cast], rest[n_cast + 1:2 * n_cast + 1]
    carry_ref, state_ref = rest[2 * n_cast + 1:]
    for src_ref, dst_ref in zip(cast_in, cast_out):
        dst_ref[...] = src_ref[...].astype(dst_ref.dtype)

    tb = p_ref.shape[0]
    c_sz = WKV_CHUNK
    n_quads = rw // QUAD
    qs = range(n_quads)
    lanes = [slice(q * QUAD, (q + 1) * QUAD) for q in qs]
    rows = HEADS_PER_QUAD * c_sz
    assert rows == QUAD

    @pl.when(pl.program_id(1) == 0)
    def _():
        carry_ref[...] = jnp.zeros_like(carry_ref)
        state_ref[...] = jnp.zeros_like(state_ref)

    ri = lax.broadcasted_iota(jnp.int32, (rows, QUAD), 0)
    ci = lax.broadcasted_iota(jnp.int32, (rows, QUAD), 1)
    ones_bd = jnp.where((ri // HEAD) == (ci // HEAD), 1.0, 0.0).astype(F32)
    avg_bd = ones_bd * (1.0 / HEAD)
    head_mask = (ri // c_sz) == (ci // HEAD)
    same = (ri // c_sz) == (ci // c_sz)
    strict = same & (ci < ri)
    incl = same & (ci <= ri)
    blocks = {}
    n = INV_LEAF
    while n <= c_sz:
        blocks[n] = (ri // n) == (ci // n)
        n *= 2
    eye = ri == ci
    eye_f = jnp.where(eye, 1.0, 0.0).astype(F32)
    crow = lax.broadcasted_iota(jnp.int32, (c_sz, 1), 0)

    def head_sums(xs):
        lhs = jnp.concatenate([x[:, lanes[q]] for x in xs for q in qs], axis=0)
        out = _mm(lhs, ones_bd)
        return [jnp.concatenate([out[(i * n_quads + q) * c_sz:(i * n_quads + q + 1) * c_sz]
                                 for q in qs], axis=1) for i in range(len(xs))]

    def b16(x):
        return x.astype(BF16)

    def stack(x):
        return jnp.where(head_mask, jnp.concatenate([b16(x)] * HEADS_PER_QUAD, axis=0), 0.0)

    def expand(x):
        return jnp.concatenate([x, x], axis=1)

    def prep_chunk(c):
        r0 = c * c_sz
        p = p_ref[r0:r0 + c_sz, :]
        before = carry_ref[0:1, :] if c == 0 else p_ref[r0 - 1:r0, :]
        prev = jnp.where(crow == 0, before, pltpu.roll(p, 1, 0))
        sh = p + (prev - p) * mu_ref[...]
        r = sh[:, 0:rw]
        k = sh[:, rw:2 * rw]
        v = sh[:, 2 * rw:3 * rw]
        xw = sh[:, win_w[0]:win_w[1]]
        xa = sh[:, win_a[0]:win_a[1]]
        xg = sh[:, win_g[0]:win_g[1]]

        wl = w0_ref[...] + _mm(jnp.tanh(xw), wup_ref[...])
        lw = (-(2.718281828459045 ** -0.5)) * jax.nn.sigmoid(wl)
        a = jax.nn.sigmoid(a0_ref[...] + _mm(xa, aup_ref[...]))
        g = _mm(jax.nn.sigmoid(xg), gup_ref[...])

        kk = k * kk_ref[...]
        k2 = k * (1.0 + (a - 1.0) * ka_ref[...])
        kk_ss, rk_sum = head_sums([kk * kk, r * k2 * rk_ref[...]])
        kk = kk * jnp.minimum(lax.rsqrt(kk_ss), 1e12)
        bonus = rk_sum * v

        cum = lw
        shift = 1
        while shift < c_sz:
            cum = cum + jnp.where(crow >= shift, pltpu.roll(cum, shift, 0), 0.0)
            shift *= 2
        ec = jnp.exp(cum)
        eci = jnp.exp(-cum)
        ec_prev = jnp.where(crow == 0, 1.0, pltpu.roll(ec, 1, 0))
        return dict(ah=-kk * ec_prev, rh=r * ec, bh=kk * a * eci, kh=k2 * eci, v=v,
                    pc=ec[c_sz - 1:c_sz, :], bonus=bonus, g=g)

    def wkv_chunk(d):
        ah = [d["ah"][:, lanes[q]] for q in qs]
        rh = [d["rh"][:, lanes[q]] for q in qs]
        bh = [d["bh"][:, lanes[q]] for q in qs]
        kh = [d["kh"][:, lanes[q]] for q in qs]
        vv = [d["v"][:, lanes[q]] for q in qs]
        pc = [d["pc"][:, lanes[q]] for q in qs]

        ah_s = [stack(x) for x in ah]
        rh_s = [stack(x) for x in rh]
        v_s = [stack(x) for x in vv]
        s = [_mm_nt(jnp.concatenate([ah_s[q], rh_s[q]], axis=0),
                    b16(jnp.concatenate([bh[q], bh[q], kh[q], kh[q]], axis=0))) for q in qs]
        a_ab = [jnp.where(strict, expand(s[q][:rows, :QUAD // 2]), 0.0) for q in qs]
        a_ak = [b16(jnp.where(strict, expand(s[q][:rows, QUAD // 2:]), 0.0)) for q in qs]
        a_rb = [b16(jnp.where(incl, expand(s[q][rows:, :QUAD // 2]), 0.0)) for q in qs]
        a_rk = [b16(jnp.where(incl, expand(s[q][rows:, QUAD // 2:]), 0.0)) for q in qs]

        a_d = [jnp.where(blocks[INV_LEAF], a_ab[q], 0.0) for q in qs]
        t_inv = [eye_f + a_d[q] for q in qs]
        pw = [b16(a_d[q]) for q in qs]
        n = 2
        while n < INV_LEAF:
            pw = [b16(_mm(pw[q], pw[q])) for q in qs]
            t_inv = [t_inv[q] + _mm(pw[q], b16(t_inv[q])) for q in qs]
            n *= 2
        n = INV_LEAF
        while n < c_sz:
            off = blocks[2 * n] & jnp.logical_not(blocks[n])
            a_off = [b16(jnp.where(off, a_ab[q], 0.0)) for q in qs]
            t16 = [b16(t_inv[q]) for q in qs]
            ta = [b16(_mm(t16[q], a_off[q])) for q in qs]
            t_inv = [t_inv[q] + _mm(ta[q], t16[q]) for q in qs]
            n *= 2

        akv = [b16(_mm(a_ak[q], v_s[q])) for q in qs]
        wu = [_mm(b16(t_inv[q]), jnp.concatenate([ah_s[q], akv[q]], axis=1)) for q in qs]
        wt = [b16(wu[q][:, :QUAD]) for q in qs]
        ut = [wu[q][:, QUAD:] for q in qs]
        bk = [jnp.concatenate([stack(bh[q] * pc[q]), stack(kh[q] * pc[q])], axis=0).T for q in qs]
        pc_col = [jnp.sum(jnp.where(eye, pc[q], 0.0), axis=1, keepdims=True) for q in qs]
        return dict(rh_s=rh_s, v_s=v_s, a_rb=a_rb, a_rk=a_rk, wt=wt, ut=ut, bk=bk, pc_col=pc_col)

    def wkv_state(w):
        rh_s, v_s, a_rb, a_rk = w["rh_s"], w["v_s"], w["a_rb"], w["a_rk"]
        wt, ut, bk, pc_col = w["wt"], w["ut"], w["bk"], w["pc_col"]
        m_st = [state_ref[q] for q in qs]
        m16 = [b16(m_st[q]) for q in qs]
        u16 = [b16(_mm(wt[q], m16[q]) + ut[q]) for q in qs]
        muv = [jnp.concatenate([m16[q], u16[q], v_s[q]], axis=0) for q in qs]
        y_s = [_mm(jnp.concatenate([rh_s[q], a_rb[q], a_rk[q]], axis=1), muv[q]) for q in qs]
        ys = []
        for q in qs:
            y = y_s[q][0:c_sz]
            for h in range(1, HEADS_PER_QUAD):
                y = y + y_s[q][h * c_sz:(h + 1) * c_sz]
            ys.append(y)
            state_ref[q] = (pc_col[q] * m_st[q]
                            + _mm(bk[q], jnp.concatenate([u16[q], v_s[q]], axis=0)))
        return ys

    def post_chunk(c, ys, d):
        y = jnp.concatenate(ys, axis=0)
        yc = y - _mm(y, avg_bd)
        var = _mm(yc * yc, avg_bd)
        yn = yc * lax.rsqrt(var + LNX_EPS)
        yn = jnp.concatenate([yn[q * c_sz:(q + 1) * c_sz] for q in qs], axis=1)
        out = (yn * lng_ref[...] + lnb_ref[...] + d["bonus"]) * d["g"]
        o_ref[c * c_sz:(c + 1) * c_sz, :] = out.astype(o_ref.dtype)

    n_chunks = tb // c_sz
    nxt = prep_chunk(0)
    nxt_w = wkv_chunk(nxt)
    for c in range(n_chunks):
        cur, cur_w = nxt, nxt_w
        if c + 1 < n_chunks:
            nxt = prep_chunk(c + 1)
            nxt_w = wkv_chunk(nxt)
        post_chunk(c, wkv_state(cur_w), cur)
    carry_ref[0:1, :] = p_ref[tb - 1:tb, :]


def _lane_window(start, stop):
    return (start // LANE) * LANE, -(-stop // LANE) * LANE


def _pad_rows_to_window(w, start, window):
    return jnp.pad(w, ((start - window[0], window[1] - start - w.shape[0]), (0, 0)))


def _rwkv_mix(p, mu, w0, w_lora_up, a0, a_lora_up, g_lora_up, k_k, k_a, r_k, lnx_g, lnx_b, cast, *,
              tb):
    bsz, t, cols = p.shape
    rw = w0.shape[1]
    o1 = 3 * rw
    o2 = o1 + w_lora_up.shape[0]
    o3 = o2 + a_lora_up.shape[0]
    o4 = o3 + g_lora_up.shape[0]
    win_w, win_a, win_g = _lane_window(o1, o2), _lane_window(o2, o3), _lane_window(o3, o4)
    assert win_g[1] == cols
    wup = _pad_rows_to_window(w_lora_up, o1, win_w)
    aup = _pad_rows_to_window(a_lora_up, o2, win_a)
    gup = _pad_rows_to_window(g_lora_up, o3, win_g)
    row_spec = lambda w: pl.BlockSpec((1, w), lambda b, i: (0, 0))
    full = lambda a: pl.BlockSpec(a.shape, lambda b, i: (0, 0))
    nt = t // tb
    steps = bsz * nt
    slabs = [_slab_specs(*c, steps, lambda b, i: b * nt + i) for c in cast]
    outs = pl.pallas_call(
        functools.partial(_rwkv_mix_kernel, rw=rw, win_w=win_w, win_a=win_a, win_g=win_g,
                          n_cast=len(cast)),
        grid=(bsz, nt),
        in_specs=[
            pl.BlockSpec((None, tb, cols), lambda b, i: (b, i, 0)),
            row_spec(cols), row_spec(rw), full(wup), row_spec(rw), full(aup), full(gup),
            row_spec(rw), row_spec(rw), row_spec(rw), row_spec(rw), row_spec(rw),
        ] + [sl[0] for sl in slabs],
        out_specs=[pl.BlockSpec((None, tb, rw), lambda b, i: (b, i, 0))] + [sl[1] for sl in slabs],
        out_shape=[jax.ShapeDtypeStruct((bsz, t, rw), BF16)] + [sl[2] for sl in slabs],
        scratch_shapes=[pltpu.VMEM((SUBLANE, cols), F32),
                        pltpu.VMEM((rw // QUAD, QUAD, QUAD), F32)],
        compiler_params=pltpu.CompilerParams(
            dimension_semantics=("parallel", "arbitrary"), vmem_limit_bytes=VMEM_LIMIT),
        name="rwkv_mix",
    )(p, mu, w0, wup, a0, aup, gup, k_k, k_a, r_k, lnx_g, lnx_b, *[a for a, _, _ in cast])
    return outs[0], outs[1:]


def _gelu(y):
    return 0.5 * y * (1.0 + lax.erf(y * (2.0 ** -0.5)))


def _front_kernel(x_ref, g_ref, wzt_ref, lng_ref, lnb_ref, ws_ref, bs_ref, cast_ref, n_ref, o_ref,
                  cast16_ref, w16_ref, *, sub):
    tm, width = o_ref.shape
    cast16_ref[...] = cast_ref[...].astype(BF16)

    @pl.when(pl.program_id(0) == 0)
    def _():
        w16_ref[...] = wzt_ref[...].astype(BF16)

    n = _rmsnorm(x_ref[...], g_ref[...]).astype(BF16)
    n_ref[...] = n

    def z_cols(c0):
        return _gelu(_mm_nt(n, w16_ref[c0:c0 + sub, :]))

    v = jnp.concatenate([z_cols(width + k * sub) for k in range(width // sub)], axis=1)
    mean = jnp.mean(v, axis=-1, keepdims=True)
    vc = v - mean
    var = jnp.mean(vc * vc, axis=-1, keepdims=True)
    vn = vc * lax.rsqrt(var + LN_EPS) * lng_ref[...] + lnb_ref[...]

    tr = lax.broadcasted_iota(jnp.int32, (SGU_CHUNK, SGU_CHUNK), 0)
    tc = lax.broadcasted_iota(jnp.int32, (SGU_CHUNK, SGU_CHUNK), 1)
    causal = tc <= tr
    for k in range(width // sub):
        u = z_cols(k * sub)
        for gi in range(sub // SGU_GROUP):
            g = k * (sub // SGU_GROUP) + gi
            w_c = jnp.where(causal, ws_ref[g], 0.0)
            lanes = slice(g * SGU_GROUP, (g + 1) * SGU_GROUP)
            ul = slice(gi * SGU_GROUP, (gi + 1) * SGU_GROUP)
            for c in range(tm // SGU_CHUNK):
                rows = slice(c * SGU_CHUNK, (c + 1) * SGU_CHUNK)
                mixed = _mm(w_c, vn[rows, lanes]) + bs_ref[:, lanes]
                o_ref[rows, lanes] = (u[rows, ul] * mixed).astype(o_ref.dtype)


def _front(x, g, w_t, row0, ln_g, ln_b, w_s, b_full, cast, *, tm, sub=256):
    m, d = x.shape
    width = ln_g.shape[1]
    cast_in, cast_out, cast_shape = _slab_specs(*cast, m // tm, lambda i: i)
    return pl.pallas_call(
        functools.partial(_front_kernel, sub=sub),
        grid=(m // tm,),
        in_specs=[
            pl.BlockSpec((tm, d), lambda i: (i, 0)),
            pl.BlockSpec((1, d), lambda i: (0, 0)),
            pl.BlockSpec((pl.Element(2 * width), pl.Element(d)), lambda i: (row0, 0),
                         pipeline_mode=pl.Buffered(1)),
            pl.BlockSpec((1, width), lambda i: (0, 0)),
            pl.BlockSpec((1, width), lambda i: (0, 0)),
            pl.BlockSpec(w_s.shape, lambda i: (0, 0, 0)),
            pl.BlockSpec(b_full.shape, lambda i: (0, 0)),
            cast_in,
        ],
        out_specs=[pl.BlockSpec((tm, d), lambda i: (i, 0)), pl.BlockSpec((tm, width), lambda i: (i, 0)),
                   cast_out],
        out_shape=[jax.ShapeDtypeStruct((m, d), BF16), jax.ShapeDtypeStruct((m, width), BF16),
                   cast_shape],
        scratch_shapes=[pltpu.VMEM((2 * width, d), BF16)],
        compiler_params=pltpu.CompilerParams(
            dimension_semantics=("arbitrary",), vmem_limit_bytes=VMEM_LIMIT),
        name="front",
    )(x, g, w_t, ln_g, ln_b, w_s, b_full, cast[0])


def _merge_kernel(n1_ref, ya_ref, yb_ref, x_ref, wgt_ref, wa_ref, wb_ref, wo_ref, g2_ref, h_ref,
                  n_ref, *, sub):
    d = x_ref.shape[1]
    n1 = n1_ref[...]
    ya = ya_ref[...]
    yb = yb_ref[...]
    parts = []
    for k in range(d // sub):
        cols = slice(k * sub, (k + 1) * sub)
        ga = jax.nn.sigmoid(_mm_nt(n1, wgt_ref[k * sub:(k + 1) * sub, :]))
        gb = jax.nn.sigmoid(_mm_nt(n1, wgt_ref[d + k * sub:d + (k + 1) * sub, :]))
        parts.append((ga * _mm(ya, wa_ref[:, cols]) + gb * _mm(yb, wb_ref[:, cols])).astype(BF16))
    h = x_ref[...] + _mm(jnp.concatenate(parts, axis=1), wo_ref[...])
    h_ref[...] = h
    n_ref[...] = _rmsnorm(h, g2_ref[...]).astype(BF16)


def _merge(n1, ya, yb, x, wgt, wa, wb, wo, g2, *, tm, sub=256):
    m, d = x.shape
    rw = ya.shape[1]
    const = lambda a: pl.BlockSpec(a.shape, lambda i: (0, 0), pipeline_mode=pl.Buffered(1))
    return pl.pallas_call(
        functools.partial(_merge_kernel, sub=sub),
        grid=(m // tm,),
        in_specs=[
            pl.BlockSpec((tm, d), lambda i: (i, 0)),
            pl.BlockSpec((tm, rw), lambda i: (i, 0)),
            pl.BlockSpec((tm, rw), lambda i: (i, 0)),
            pl.BlockSpec((tm, d), lambda i: (i, 0)),
            const(wgt), const(wa), const(wb), const(wo),
            pl.BlockSpec((1, d), lambda i: (0, 0)),
        ],
        out_specs=[pl.BlockSpec((tm, d), lambda i: (i, 0)), pl.BlockSpec((tm, d), lambda i: (i, 0))],
        out_shape=[jax.ShapeDtypeStruct((m, d), F32), jax.ShapeDtypeStruct((m, d), BF16)],
        compiler_params=pltpu.CompilerParams(
            dimension_semantics=("parallel",), vmem_limit_bytes=VMEM_LIMIT),
        name="merge",
    )(n1, ya, yb, x, wgt, wa, wb, wo, g2)


def _ffn_kernel(n_ref, h_ref, wg_ref, wu_ref, wd_ref, gf_ref, o_ref, *, sub):
    j = pl.program_id(1)

    @pl.when(j == 0)
    def _():
        o_ref[...] = h_ref[...]

    n = n_ref[...]
    part = None
    for k in range(wg_ref.shape[1] // sub):
        cols = slice(k * sub, (k + 1) * sub)
        gate = _mm(n, wg_ref[:, cols])
        up = _mm(n, wu_ref[:, cols])
        act = (gate * jax.nn.sigmoid(gate) * up).astype(BF16)
        down = _mm(act, wd_ref[cols, :])
        part = down if part is None else part + down
    o_ref[...] += part

    @pl.when(j == pl.num_programs(1) - 1)
    def _():
        o_ref[...] = _rmsnorm(o_ref[...], gf_ref[...])


def _ffn(n2, h1, wg, wu, wd, gf, *, tm, th, sub=256):
    m, d = h1.shape
    hid = wg.shape[1]
    return pl.pallas_call(
        functools.partial(_ffn_kernel, sub=sub),
        grid=(m // tm, hid // th),
        in_specs=[
            pl.BlockSpec((tm, d), lambda i, j: (i, 0)),
            pl.BlockSpec((tm, d), lambda i, j: (i, 0)),
            pl.BlockSpec((d, th), lambda i, j: (0, j)),
            pl.BlockSpec((d, th), lambda i, j: (0, j)),
            pl.BlockSpec((th, d), lambda i, j: (j, 0)),
            pl.BlockSpec((1, d), lambda i, j: (0, 0)),
        ],
        out_specs=pl.BlockSpec((tm, d), lambda i, j: (i, 0)),
        out_shape=jax.ShapeDtypeStruct((m, d), F32),
        compiler_params=pltpu.CompilerParams(
            dimension_semantics=("parallel", "arbitrary"), vmem_limit_bytes=VMEM_LIMIT),
        name="ffn",
    )(n2, h1, wg, wu, wd, gf)


def _pad_cols(a, width):
    return jnp.pad(a, ((0, 0), (0, width - a.shape[1])))


def _block(h, norm_mix_g, w_in, shift_mu, w0, w_lora_up, a0, a_lora_up, g_lora_up, k_k, k_a, r_k,
           lnx_g, lnx_b, w_proj_rwkv, sgu_ln_g, sgu_ln_b, sgu_w, sgu_b, w_proj_sgu, w_out,
           norm_ffn_g, w_ffn_gate, w_ffn_up, w_ffn_down, g_last, *, bsz, seq):
    m, d = h.shape
    rw = w0.shape[0]
    sw = sgu_ln_g.shape[0]
    rcols = shift_mu.shape[0]
    zc = rcols + 2 * sw
    rcols_al = -(-rcols // LANE) * LANE
    row = lambda a: a.reshape(1, -1)

    w_t = w_in.T
    mu = _pad_cols(shift_mu[None, :], rcols_al)

    b_full = jnp.repeat(sgu_b.T, SGU_GROUP, axis=1)
    n1, yb, w_rwkv16t = _front(h, row(norm_mix_g), w_t, rcols, row(sgu_ln_g), row(sgu_ln_b), sgu_w,
                               b_full, (w_t, 0, rcols_al), tm=512)
    p = _proj(n1, w_rwkv16t, tm=512)

    whole = lambda a: (a, 0, a.shape[0])
    ya, (wgt, wa, wb, wo, wg, wu, wd) = _rwkv_mix(
        p.reshape(bsz, seq, -1), mu, row(w0), w_lora_up, row(a0), a_lora_up, g_lora_up,
        row(k_k), row(k_a), row(r_k), row(lnx_g), row(lnx_b),
        ((w_t, zc, 2 * d), whole(w_proj_rwkv), whole(w_proj_sgu), whole(w_out),
         whole(w_ffn_gate), whole(w_ffn_up), whole(w_ffn_down)), tb=256)

    h1, n2 = _merge(n1, ya.reshape(m, rw), yb, h, wgt, wa, wb, wo, row(norm_ffn_g), tm=256)
    return _ffn(n2, h1, wg, wu, wd, row(g_last), tm=1024, th=512)


def kernel(x, norm_mix_g, w_in, shift_mu, w0, w_lora_up, a0, a_lora_up, g_lora_up, k_k, k_a, r_k,
           lnx_g, lnx_b, w_proj_rwkv, sgu_ln_g, sgu_ln_b, sgu_w, sgu_b, w_proj_sgu, w_out,
           norm_ffn_g, w_ffn_gate, w_ffn_up, w_ffn_down, norm_final_g):
    bsz, seq, d = x.shape
    depth = w_in.shape[0]
    assert depth == 1, "the final rmsnorm is fused into the (single) layer's channel mixer"
    h = x.reshape(bsz * seq, d)
    out = _block(h, norm_mix_g[0], w_in[0], shift_mu[0], w0[0], w_lora_up[0], a0[0], a_lora_up[0],
                 g_lora_up[0], k_k[0], k_a[0], r_k[0], lnx_g[0], lnx_b[0], w_proj_rwkv[0],
                 sgu_ln_g[0], sgu_ln_b[0], sgu_w[0], sgu_b[0], w_proj_sgu[0], w_out[0],
                 norm_ffn_g[0], w_ffn_gate[0], w_ffn_up[0], w_ffn_down[0], norm_final_g,
                 bsz=bsz, seq=seq)
    return out.reshape(bsz, seq, d)
```

```python
import functools

import jax
import jax.numpy as jnp
from jax import lax
from jax.experimental import pallas as pl
from jax.experimental.pallas import tpu as pltpu

F32 = jnp.float32
BF16 = jnp.bfloat16

RMS_EPS = 1e-6
LN_EPS = 1e-5
LNX_EPS = 64e-5

HEAD = 64
WKV_CHUNK = 64
INV_LEAF = 8
QUAD = 256
HEADS_PER_QUAD = QUAD // HEAD
SGU_CHUNK = 128
SGU_GROUP = 128
LANE = 128
SUBLANE = 8

VMEM_LIMIT = 60 * 1024 * 1024


def _mm(a, b):
    return jnp.dot(a, b, preferred_element_type=F32)


def _mm_nt(a, b):
    return lax.dot_general(a, b, (((1,), (1,)), ((), ())), preferred_element_type=F32)


def _rmsnorm(x, g):
    return x * lax.rsqrt(jnp.mean(x * x, axis=-1, keepdims=True) + RMS_EPS) * g


def _proj_kernel(n_ref, w16t_ref, o_ref, *, sub):
    for k in range(o_ref.shape[1] // sub):
        cols = slice(k * sub, (k + 1) * sub)
        o_ref[:, cols] = _mm_nt(n_ref[...], w16t_ref[cols, :])


def _proj(n, w16t, *, tm, sub=256):
    m, d = n.shape
    n_cols = w16t.shape[0]
    return pl.pallas_call(
        functools.partial(_proj_kernel, sub=sub),
        grid=(m // tm,),
        in_specs=[
            pl.BlockSpec((tm, d), lambda i: (i, 0)),
            pl.BlockSpec((n_cols, d), lambda i: (0, 0), pipeline_mode=pl.Buffered(1)),
        ],
        out_specs=pl.BlockSpec((tm, n_cols), lambda i: (i, 0)),
        out_shape=jax.ShapeDtypeStruct((m, n_cols), F32),
        compiler_params=pltpu.CompilerParams(
            dimension_semantics=("parallel",), vmem_limit_bytes=VMEM_LIMIT),
        name="proj",
    )(n, w16t)


def _slab_specs(a, row0, nrows, steps, step_index):
    assert row0 % SUBLANE == 0 and nrows % (steps * 2 * SUBLANE) == 0
    r = nrows // steps
    cols = a.shape[1]
    in_spec = pl.BlockSpec(
        (pl.Element(r), pl.Element(cols)),
        lambda *g: (pl.multiple_of(row0 + step_index(*g) * r, SUBLANE), 0))
    out_spec = pl.BlockSpec((r, cols), lambda *g: (step_index(*g), 0))
    return in_spec, out_spec, jax.ShapeDtypeStruct((nrows, cols), BF16)


def _rwkv_mix_kernel(p_ref, mu_ref, w0_ref, wup_ref, a0_ref, aup_ref, gup_ref, kk_ref, ka_ref,
                     rk_ref, lng_ref, lnb_ref, *rest, rw, win_w, win_a, win_g, n_cast):
    cast_in, o_ref, cast_out = rest[:n_cast], rest[n_cast], rest[n_cast + 1:2 * n_cast + 1]
    carry_ref, state_ref = rest[2 * n_cast + 1:]
    for src_ref, dst_ref in zip(cast_in, cast_out):
        dst_ref[...] = src_ref[...].astype(dst_ref.dtype)

    tb = p_ref.shape[0]
    c_sz = WKV_CHUNK
    n_quads = rw // QUAD
    qs = range(n_quads)
    lanes = [slice(q * QUAD, (q + 1) * QUAD) for q in qs]
    rows = HEADS_PER_QUAD * c_sz
    assert rows == QUAD

    @pl.when(pl.program_id(1) == 0)
    def _():
        carry_ref[...] = jnp.zeros_like(carry_ref)
        state_ref[...] = jnp.zeros_like(state_ref)

    ri = lax.broadcasted_iota(jnp.int32, (rows, QUAD), 0)
    ci = lax.broadcasted_iota(jnp.int32, (rows, QUAD), 1)
    ones_bd = jnp.where((ri // HEAD) == (ci // HEAD), 1.0, 0.0).astype(F32)
    avg_bd = ones_bd * (1.0 / HEAD)
    head_mask = (ri // c_sz) == (ci // HEAD)
    same = (ri // c_sz) == (ci // c_sz)
    strict = same & (ci < ri)
    incl = same & (ci <= ri)
    blocks = {}
    n = INV_LEAF
    while n <= c_sz:
        blocks[n] = (ri // n) == (ci // n)
        n *= 2
    eye = ri == ci
    eye_f = jnp.where(eye, 1.0, 0.0).astype(F32)
    crow = lax.broadcasted_iota(jnp.int32, (c_sz, 1), 0)

    def head_sums(xs):
        lhs = jnp.concatenate([x[:, lanes[q]] for x in xs for q in qs], axis=0)
        out = _mm(lhs, qz(ones_bd))
        return [jnp.concatenate([out[(i * n_quads + q) * c_sz:(i * n_quads + q + 1) * c_sz]
                                 for q in qs], axis=1) for i in range(len(xs))]

    def b16(x):
        return x.astype(BF16)

    def stack(x):
        return qz(jnp.where(head_mask, jnp.concatenate([b16(x)] * HEADS_PER_QUAD, axis=0), 0.0))

    half = QUAD // 2

    def block_diag(a, b):
        zero_q = jnp.zeros((half, half), a.dtype)
        return jnp.concatenate([jnp.concatenate([a, zero_q], axis=1),
                                jnp.concatenate([zero_q, b], axis=1)], axis=0)

    def qz(x):
        return block_diag(x[:half, :half], x[half:, half:])

    def expand(x):
        return jnp.concatenate([x, x], axis=1)

    lo = 3 * rw
    p_l = p_ref[:, lo:]
    brow = lax.broadcasted_iota(jnp.int32, (tb, 1), 0)
    prev_l = jnp.where(brow == 0, carry_ref[0:1, lo:], pltpu.roll(p_l, 1, 0))
    sh_l = p_l + (prev_l - p_l) * mu_ref[:, lo:]

    def lora(act, win, start, w_ref):
        before, after = start - win[0], win[1] - start - w_ref.shape[0]
        assert before % SUBLANE == 0 and after % SUBLANE == 0
        pieces = [jnp.zeros((before, rw), F32), w_ref[...], jnp.zeros((after, rw), F32)]
        w = jnp.concatenate([x for x in pieces if x.shape[0]], axis=0)
        return _mm(act(sh_l[:, win[0] - lo:win[1] - lo]), w)

    o_a = lo + wup_ref.shape[0]
    o_g = o_a + aup_ref.shape[0]
    wl_all = w0_ref[...] + lora(jnp.tanh, win_w, lo, wup_ref)
    apre_all = a0_ref[...] + lora(lambda x: x, win_a, o_a, aup_ref)
    g_all = lora(jax.nn.sigmoid, win_g, o_g, gup_ref)

    def prep_chunk(c):
        r0 = c * c_sz
        p = p_ref[r0:r0 + c_sz, :lo]
        before = carry_ref[0:1, :lo] if c == 0 else p_ref[r0 - 1:r0, :lo]
        prev = jnp.where(crow == 0, before, pltpu.roll(p, 1, 0))
        sh = p + (prev - p) * mu_ref[:, :lo]
        r = sh[:, 0:rw]
        k = sh[:, rw:2 * rw]
        v = sh[:, 2 * rw:3 * rw]

        lw = (-(2.718281828459045 ** -0.5)) * jax.nn.sigmoid(wl_all[r0:r0 + c_sz])
        a = jax.nn.sigmoid(apre_all[r0:r0 + c_sz])
        g = g_all[r0:r0 + c_sz]

        kk = k * kk_ref[...]
        k2 = k * (1.0 + (a - 1.0) * ka_ref[...])
        kk_ss, rk_sum = head_sums([kk * kk, r * k2 * rk_ref[...]])
        kk = kk * jnp.minimum(lax.rsqrt(kk_ss), 1e12)
        bonus = rk_sum * v

        cum = lw
        shift = 1
        while shift < c_sz:
            cum = cum + jnp.where(crow >= shift, pltpu.roll(cum, shift, 0), 0.0)
            shift *= 2
        ec = jnp.exp(cum)
        eci = jnp.exp(-cum)
        ec_prev = jnp.where(crow == 0, 1.0, pltpu.roll(ec, 1, 0))
        return dict(ah=-kk * ec_prev, rh=r * ec, bh=kk * a * eci, kh=k2 * eci, v=v,
                    pc=ec[c_sz - 1:c_sz, :], bonus=bonus, g=g)

    def wkv_chunk(d):
        ah = [d["ah"][:, lanes[q]] for q in qs]
        rh = [d["rh"][:, lanes[q]] for q in qs]
        bh = [d["bh"][:, lanes[q]] for q in qs]
        kh = [d["kh"][:, lanes[q]] for q in qs]
        vv = [d["v"][:, lanes[q]] for q in qs]
        pc = [d["pc"][:, lanes[q]] for q in qs]

        ah_s = [stack(x) for x in ah]
        rh_s = [stack(x) for x in rh]
        v_s = [stack(x) for x in vv]
        s = [_mm_nt(jnp.concatenate([ah_s[q], rh_s[q]], axis=0),
                    b16(jnp.concatenate([bh[q], bh[q], kh[q], kh[q]], axis=0))) for q in qs]
        a_ab = [qz(jnp.where(strict, expand(s[q][:rows, :QUAD // 2]), 0.0)) for q in qs]
        a_ak = [qz(b16(jnp.where(strict, expand(s[q][:rows, QUAD // 2:]), 0.0))) for q in qs]
        a_rb = [qz(b16(jnp.where(incl, expand(s[q][rows:, :QUAD // 2]), 0.0))) for q in qs]
        a_rk = [qz(b16(jnp.where(incl, expand(s[q][rows:, QUAD // 2:]), 0.0))) for q in qs]

        a_d = [qz(jnp.where(blocks[INV_LEAF], a_ab[q], 0.0)) for q in qs]
        t_inv = [qz(eye_f + a_d[q]) for q in qs]
        pw = [b16(a_d[q]) for q in qs]
        n = 2
        while n < INV_LEAF:
            pw = [qz(b16(_mm(pw[q], pw[q]))) for q in qs]
            t_inv = [qz(t_inv[q] + _mm(pw[q], b16(t_inv[q]))) for q in qs]
            n *= 2
        n = INV_LEAF
        while n < c_sz:
            off = blocks[2 * n] & jnp.logical_not(blocks[n])
            a_off = [qz(b16(jnp.where(off, a_ab[q], 0.0))) for q in qs]
            t16 = [b16(t_inv[q]) for q in qs]
            ta = [qz(b16(_mm(t16[q], a_off[q]))) for q in qs]
            t_inv = [qz(t_inv[q] + _mm(ta[q], t16[q])) for q in qs]
            n *= 2

        akv = [qz(b16(_mm(a_ak[q], v_s[q]))) for q in qs]
        t16 = [b16(t_inv[q]) for q in qs]
        wu_a = [_mm(t16[q][:half, :half],
                    jnp.concatenate([ah_s[q][:half, :half], akv[q][:half, :half]], axis=1)) for q in qs]
        wu_b = [_mm(t16[q][half:, half:],
                    jnp.concatenate([ah_s[q][half:, half:], akv[q][half:, half:]], axis=1)) for q in qs]
        wt = [block_diag(b16(wu_a[q][:, :half]), b16(wu_b[q][:, :half])) for q in qs]
        ut = [block_diag(wu_a[q][:, half:], wu_b[q][:, half:]) for q in qs]
        bk = [jnp.concatenate([qz(stack(bh[q] * pc[q]).T), qz(stack(kh[q] * pc[q]).T)], axis=1)
              for q in qs]
        pc_col = [jnp.sum(jnp.where(eye, pc[q], 0.0), axis=1, keepdims=True) for q in qs]
        return dict(rh_s=rh_s, v_s=v_s, a_rb=a_rb, a_rk=a_rk, wt=wt, ut=ut, bk=bk, pc_col=pc_col)

    def wkv_state(w):
        rh_s, v_s, a_rb, a_rk = w["rh_s"], w["v_s"], w["a_rb"], w["a_rk"]
        wt, ut, bk, pc_col = w["wt"], w["ut"], w["bk"], w["pc_col"]
        m_st = [state_ref[q] for q in qs]
        m16 = [qz(b16(m_st[q])) for q in qs]
        u16 = [qz(b16(_mm(wt[q], m16[q]) + ut[q])) for q in qs]
        muv = [jnp.concatenate([m16[q], u16[q], v_s[q]], axis=0) for q in qs]
        y_s = [_mm(jnp.concatenate([rh_s[q], a_rb[q], a_rk[q]], axis=1), muv[q]) for q in qs]
        ys = []
        for q in qs:
            ys.append(jnp.concatenate(
                [y_s[q][0:c_sz, :half] + y_s[q][c_sz:2 * c_sz, :half],
                 y_s[q][2 * c_sz:3 * c_sz, half:] + y_s[q][3 * c_sz:, half:]], axis=1))
            state_ref[q] = (pc_col[q] * m_st[q]
                            + _mm(bk[q], jnp.concatenate([u16[q], v_s[q]], axis=0)))
        return ys

    def post_chunk(c, ys, d):
        y = jnp.concatenate(ys, axis=0)
        yc = y - _mm(y, qz(avg_bd))
        var = _mm(yc * yc, qz(avg_bd))
        yn = yc * lax.rsqrt(var + LNX_EPS)
        yn = jnp.concatenate([yn[q * c_sz:(q + 1) * c_sz] for q in qs], axis=1)
        out = (yn * lng_ref[...] + lnb_ref[...] + d["bonus"]) * d["g"]
        o_ref[c * c_sz:(c + 1) * c_sz, :] = out.astype(o_ref.dtype)

    n_chunks = tb // c_sz
    nxt = prep_chunk(0)
    nxt_w = wkv_chunk(nxt)
    for c in range(n_chunks):
        cur, cur_w = nxt, nxt_w
        if c + 1 < n_chunks:
            nxt = prep_chunk(c + 1)
            nxt_w = wkv_chunk(nxt)
        post_chunk(c, wkv_state(cur_w), cur)
    carry_ref[0:1, :] = p_ref[tb - 1:tb, :]


def _lane_window(start, stop):
    return (start // LANE) * LANE, -(-stop // LANE) * LANE


def _rwkv_mix(p, mu, w0, w_lora_up, a0, a_lora_up, g_lora_up, k_k, k_a, r_k, lnx_g, lnx_b, cast, *,
              tb):
    bsz, t, cols = p.shape
    rw = w0.shape[1]
    o1 = 3 * rw
    o2 = o1 + w_lora_up.shape[0]
    o3 = o2 + a_lora_up.shape[0]
    o4 = o3 + g_lora_up.shape[0]
    win_w, win_a, win_g = _lane_window(o1, o2), _lane_window(o2, o3), _lane_window(o3, o4)
    assert win_g[1] == cols
    wup, aup, gup = w_lora_up, a_lora_up, g_lora_up
    row_spec = lambda w: pl.BlockSpec((1, w), lambda b, i: (0, 0))
    full = lambda a: pl.BlockSpec(a.shape, lambda b, i: (0, 0))
    nt = t // tb
    steps = bsz * nt
    slabs = [_slab_specs(*c, steps, lambda b, i: b * nt + i) for c in cast]
    outs = pl.pallas_call(
        functools.partial(_rwkv_mix_kernel, rw=rw, win_w=win_w, win_a=win_a, win_g=win_g,
                          n_cast=len(cast)),
        grid=(bsz, nt),
        in_specs=[
            pl.BlockSpec((None, tb, cols), lambda b, i: (b, i, 0)),
            row_spec(cols), row_spec(rw), full(wup), row_spec(rw), full(aup), full(gup),
            row_spec(rw), row_spec(rw), row_spec(rw), row_spec(rw), row_spec(rw),
        ] + [sl[0] for sl in slabs],
        out_specs=[pl.BlockSpec((None, tb, rw), lambda b, i: (b, i, 0))] + [sl[1] for sl in slabs],
        out_shape=[jax.ShapeDtypeStruct((bsz, t, rw), BF16)] + [sl[2] for sl in slabs],
        scratch_shapes=[pltpu.VMEM((SUBLANE, cols), F32),
                        pltpu.VMEM((rw // QUAD, QUAD, QUAD), F32)],
        compiler_params=pltpu.CompilerParams(
            dimension_semantics=("parallel", "arbitrary"), vmem_limit_bytes=VMEM_LIMIT),
        name="rwkv_mix",
    )(p, mu, w0, wup, a0, aup, gup, k_k, k_a, r_k, lnx_g, lnx_b, *[a for a, _, _ in cast])
    return outs[0], outs[1:]


def _gelu(y):
    return 0.5 * y * (1.0 + lax.erf(y * (2.0 ** -0.5)))


def _front_kernel(x_ref, g_ref, wzt_ref, lng_ref, lnb_ref, ws_ref, bs_ref, cast_ref, n_ref, o_ref,
                  cast16_ref, w16_ref, *, sub):
    tm, width = o_ref.shape
    cast16_ref[...] = cast_ref[...].astype(BF16)

    @pl.when(pl.program_id(0) == 0)
    def _():
        w16_ref[...] = wzt_ref[...].astype(BF16)

    n = _rmsnorm(x_ref[...], g_ref[...]).astype(BF16)
    n_ref[...] = n

    def z_cols(c0):
        return _gelu(_mm_nt(n, w16_ref[c0:c0 + sub, :]))

    v = jnp.concatenate([z_cols(width + k * sub) for k in range(width // sub)], axis=1)
    mean = jnp.mean(v, axis=-1, keepdims=True)
    vc = v - mean
    var = jnp.mean(vc * vc, axis=-1, keepdims=True)
    vn = vc * lax.rsqrt(var + LN_EPS) * lng_ref[...] + lnb_ref[...]

    tr = lax.broadcasted_iota(jnp.int32, (SGU_CHUNK, SGU_CHUNK), 0)
    tc = lax.broadcasted_iota(jnp.int32, (SGU_CHUNK, SGU_CHUNK), 1)
    causal = tc <= tr
    for k in range(width // sub):
        u = z_cols(k * sub)
        for gi in range(sub // SGU_GROUP):
            g = k * (sub // SGU_GROUP) + gi
            w_c = jnp.where(causal, ws_ref[g], 0.0)
            lanes = slice(g * SGU_GROUP, (g + 1) * SGU_GROUP)
            ul = slice(gi * SGU_GROUP, (gi + 1) * SGU_GROUP)
            for c in range(tm // SGU_CHUNK):
                rows = slice(c * SGU_CHUNK, (c + 1) * SGU_CHUNK)
                mixed = _mm(w_c, vn[rows, lanes]) + bs_ref[:, lanes]
                o_ref[rows, lanes] = (u[rows, ul] * mixed).astype(o_ref.dtype)


def _front(x, g, w_t, row0, ln_g, ln_b, w_s, b_full, cast, *, tm, sub=256):
    m, d = x.shape
    width = ln_g.shape[1]
    cast_in, cast_out, cast_shape = _slab_specs(*cast, m // tm, lambda i: i)
    return pl.pallas_call(
        functools.partial(_front_kernel, sub=sub),
        grid=(m // tm,),
        in_specs=[
            pl.BlockSpec((tm, d), lambda i: (i, 0)),
            pl.BlockSpec((1, d), lambda i: (0, 0)),
            pl.BlockSpec((pl.Element(2 * width), pl.Element(d)), lambda i: (row0, 0),
                         pipeline_mode=pl.Buffered(1)),
            pl.BlockSpec((1, width), lambda i: (0, 0)),
            pl.BlockSpec((1, width), lambda i: (0, 0)),
            pl.BlockSpec(w_s.shape, lambda i: (0, 0, 0)),
            pl.BlockSpec(b_full.shape, lambda i: (0, 0)),
            cast_in,
        ],
        out_specs=[pl.BlockSpec((tm, d), lambda i: (i, 0)), pl.BlockSpec((tm, width), lambda i: (i, 0)),
                   cast_out],
        out_shape=[jax.ShapeDtypeStruct((m, d), BF16), jax.ShapeDtypeStruct((m, width), BF16),
                   cast_shape],
        scratch_shapes=[pltpu.VMEM((2 * width, d), BF16)],
        compiler_params=pltpu.CompilerParams(
            dimension_semantics=("arbitrary",), vmem_limit_bytes=VMEM_LIMIT),
        name="front",
    )(x, g, w_t, ln_g, ln_b, w_s, b_full, cast[0])


def _merge_kernel(n1_ref, ya_ref, yb_ref, x_ref, wgt_ref, wa_ref, wb_ref, wo_ref, g2_ref, h_ref,
                  n_ref, *, sub):
    d = x_ref.shape[1]
    n1 = n1_ref[...]
    ya = ya_ref[...]
    yb = yb_ref[...]
    parts = []
    for k in range(d // sub):
        cols = slice(k * sub, (k + 1) * sub)
        ga = jax.nn.sigmoid(_mm_nt(n1, wgt_ref[k * sub:(k + 1) * sub, :]))
        gb = jax.nn.sigmoid(_mm_nt(n1, wgt_ref[d + k * sub:d + (k + 1) * sub, :]))
        parts.append((ga * _mm(ya, wa_ref[:, cols]) + gb * _mm(yb, wb_ref[:, cols])).astype(BF16))
    h = x_ref[...] + _mm(jnp.concatenate(parts, axis=1), wo_ref[...])
    h_ref[...] = h
    n_ref[...] = _rmsnorm(h, g2_ref[...]).astype(BF16)


def _merge(n1, ya, yb, x, wgt, wa, wb, wo, g2, *, tm, sub=256):
    m, d = x.shape
    rw = ya.shape[1]
    const = lambda a: pl.BlockSpec(a.shape, lambda i: (0, 0), pipeline_mode=pl.Buffered(1))
    return pl.pallas_call(
        functools.partial(_merge_kernel, sub=sub),
        grid=(m // tm,),
        in_specs=[
            pl.BlockSpec((tm, d), lambda i: (i, 0)),
            pl.BlockSpec((tm, rw), lambda i: (i, 0)),
            pl.BlockSpec((tm, rw), lambda i: (i, 0)),
            pl.BlockSpec((tm, d), lambda i: (i, 0)),
            const(wgt), const(wa), const(wb), const(wo),
            pl.BlockSpec((1, d), lambda i: (0, 0)),
        ],
        out_specs=[pl.BlockSpec((tm, d), lambda i: (i, 0)), pl.BlockSpec((tm, d), lambda i: (i, 0))],
        out_shape=[jax.ShapeDtypeStruct((m, d), F32), jax.ShapeDtypeStruct((m, d), BF16)],
        compiler_params=pltpu.CompilerParams(
            dimension_semantics=("parallel",), vmem_limit_bytes=VMEM_LIMIT),
        name="merge",
    )(n1, ya, yb, x, wgt, wa, wb, wo, g2)


def _ffn_kernel(n_ref, h_ref, wg_ref, wu_ref, wd_ref, gf_ref, o_ref, *, sub):
    j = pl.program_id(1)

    @pl.when(j == 0)
    def _():
        o_ref[...] = h_ref[...]

    n = n_ref[...]
    part = None
    for k in range(wg_ref.shape[1] // sub):
        cols = slice(k * sub, (k + 1) * sub)
        gate = _mm(n, wg_ref[:, cols])
        up = _mm(n, wu_ref[:, cols])
        act = (gate * jax.nn.sigmoid(gate) * up).astype(BF16)
        down = _mm(act, wd_ref[cols, :])
        part = down if part is None else part + down
    o_ref[...] += part

    @pl.when(j == pl.num_programs(1) - 1)
    def _():
        o_ref[...] = _rmsnorm(o_ref[...], gf_ref[...])


def _ffn(n2, h1, wg, wu, wd, gf, *, tm, th, sub=256):
    m, d = h1.shape
    hid = wg.shape[1]
    return pl.pallas_call(
        functools.partial(_ffn_kernel, sub=sub),
        grid=(m // tm, hid // th),
        in_specs=[
            pl.BlockSpec((tm, d), lambda i, j: (i, 0)),
            pl.BlockSpec((tm, d), lambda i, j: (i, 0)),
            pl.BlockSpec((d, th), lambda i, j: (0, j)),
            pl.BlockSpec((d, th), lambda i, j: (0, j)),
            pl.BlockSpec((th, d), lambda i, j: (j, 0)),
            pl.BlockSpec((1, d), lambda i, j: (0, 0)),
        ],
        out_specs=pl.BlockSpec((tm, d), lambda i, j: (i, 0)),
        out_shape=jax.ShapeDtypeStruct((m, d), F32),
        compiler_params=pltpu.CompilerParams(
            dimension_semantics=("parallel", "arbitrary"), vmem_limit_bytes=VMEM_LIMIT),
        name="ffn",
    )(n2, h1, wg, wu, wd, gf)


def _pad_cols(a, width):
    return jnp.pad(a, ((0, 0), (0, width - a.shape[1])))


def _block(h, norm_mix_g, w_in, shift_mu, w0, w_lora_up, a0, a_lora_up, g_lora_up, k_k, k_a, r_k,
           lnx_g, lnx_b, w_proj_rwkv, sgu_ln_g, sgu_ln_b, sgu_w, sgu_b, w_proj_sgu, w_out,
           norm_ffn_g, w_ffn_gate, w_ffn_up, w_ffn_down, g_last, *, bsz, seq):
    m, d = h.shape
    rw = w0.shape[0]
    sw = sgu_ln_g.shape[0]
    rcols = shift_mu.shape[0]
    zc = rcols + 2 * sw
    rcols_al = -(-rcols // LANE) * LANE
    row = lambda a: a.reshape(1, -1)

    w_t = w_in.T
    mu = _pad_cols(shift_mu[None, :], rcols_al)

    b_full = jnp.repeat(sgu_b.T, SGU_GROUP, axis=1)
    n1, yb, w_rwkv16t = _front(h, row(norm_mix_g), w_t, rcols, row(sgu_ln_g), row(sgu_ln_b), sgu_w,
                               b_full, (w_t, 0, rcols_al), tm=512)
    p = _proj(n1, w_rwkv16t, tm=1024)

    whole = lambda a: (a, 0, a.shape[0])
    ya, (wgt, wa, wb, wo, wg, wu, wd) = _rwkv_mix(
        p.reshape(bsz, seq, -1), mu, row(w0), w_lora_up, row(a0), a_lora_up, g_lora_up,
        row(k_k), row(k_a), row(r_k), row(lnx_g), row(lnx_b),
        ((w_t, zc, 2 * d), whole(w_proj_rwkv), whole(w_proj_sgu), whole(w_out),
         whole(w_ffn_gate), whole(w_ffn_up), whole(w_ffn_down)), tb=256)

    h1, n2 = _merge(n1, ya.reshape(m, rw), yb, h, wgt, wa, wb, wo, row(norm_ffn_g), tm=256)
    return _ffn(n2, h1, wg, wu, wd, row(g_last), tm=1024, th=512)


def kernel(x, norm_mix_g, w_in, shift_mu, w0, w_lora_up, a0, a_lora_up, g_lora_up, k_k, k_a, r_k,
           lnx_g, lnx_b, w_proj_rwkv, sgu_ln_g, sgu_ln_b, sgu_w, sgu_b, w_proj_sgu, w_out,
           norm_ffn_g, w_ffn_gate, w_ffn_up, w_ffn_down, norm_final_g):
    bsz, seq, d = x.shape
    depth = w_in.shape[0]
    assert depth == 1, "the final rmsnorm is fused into the (single) layer's channel mixer"
    h = x.reshape(bsz * seq, d)
    out = _block(h, norm_mix_g[0], w_in[0], shift_mu[0], w0[0], w_lora_up[0], a0[0], a_lora_up[0],
                 g_lora_up[0], k_k[0], k_a[0], r_k[0], lnx_g[0], lnx_b[0], w_proj_rwkv[0],
                 sgu_ln_g[0], sgu_ln_b[0], sgu_w[0], sgu_b[0], w_proj_sgu[0], w_out[0],
                 norm_ffn_g[0], w_ffn_gate[0], w_ffn_up[0], w_ffn_down[0], norm_final_g,
                 bsz=bsz, seq=seq)
    return out.reshape(bsz, seq, d)
```

```python
import functools

import jax
import jax.numpy as jnp
from jax import lax
from jax.experimental import pallas as pl
from jax.experimental.pallas import tpu as pltpu

F32 = jnp.float32
BF16 = jnp.bfloat16

RMS_EPS = 1e-6
LN_EPS = 1e-5
LNX_EPS = 64e-5

HEAD = 64
WKV_CHUNK = 64
INV_LEAF = 8
QUAD = 256
HEADS_PER_QUAD = QUAD // HEAD
SGU_CHUNK = 128
SGU_GROUP = 128
LANE = 128
SUBLANE = 8

VMEM_LIMIT = 60 * 1024 * 1024


def _mm(a, b):
    return jnp.dot(a, b, preferred_element_type=F32)


def _mm_nt(a, b):
    return lax.dot_general(a, b, (((1,), (1,)), ((), ())), preferred_element_type=F32)


def _rmsnorm(x, g):
    return x * lax.rsqrt(jnp.mean(x * x, axis=-1, keepdims=True) + RMS_EPS) * g


def _proj_kernel(n_ref, w16t_ref, o_ref, *, sub):
    for k in range(o_ref.shape[1] // sub):
        cols = slice(k * sub, (k + 1) * sub)
        o_ref[:, cols] = _mm_nt(n_ref[...], w16t_ref[cols, :])


def _proj(n, w16t, *, tm, sub=256):
    m, d = n.shape
    n_cols = w16t.shape[0]
    return pl.pallas_call(
        functools.partial(_proj_kernel, sub=sub),
        grid=(m // tm,),
        in_specs=[
            pl.BlockSpec((tm, d), lambda i: (i, 0)),
            pl.BlockSpec((n_cols, d), lambda i: (0, 0), pipeline_mode=pl.Buffered(1)),
        ],
        out_specs=pl.BlockSpec((tm, n_cols), lambda i: (i, 0)),
        out_shape=jax.ShapeDtypeStruct((m, n_cols), F32),
        compiler_params=pltpu.CompilerParams(
            dimension_semantics=("parallel",), vmem_limit_bytes=VMEM_LIMIT),
        name="proj",
    )(n, w16t)


def _slab_specs(a, row0, nrows, steps, step_index):
    assert row0 % SUBLANE == 0 and nrows % (steps * 2 * SUBLANE) == 0
    r = nrows // steps
    cols = a.shape[1]
    in_spec = pl.BlockSpec(
        (pl.Element(r), pl.Element(cols)),
        lambda *g: (pl.multiple_of(row0 + step_index(*g) * r, SUBLANE), 0))
    out_spec = pl.BlockSpec((r, cols), lambda *g: (step_index(*g), 0))
    return in_spec, out_spec, jax.ShapeDtypeStruct((nrows, cols), BF16)


def _rwkv_mix_kernel(p_ref, mu_ref, w0_ref, wup_ref, a0_ref, aup_ref, gup_ref, kk_ref, ka_ref,
                     rk_ref, lng_ref, lnb_ref, *rest, rw, win_w, win_a, win_g, n_cast):
    cast_in, o_ref, cast_out = rest[:n_cast], rest[n_cast], rest[n_cast + 1:2 * n_cast + 1]
    carry_ref, state_ref = rest[2 * n_cast + 1:]
    for src_ref, dst_ref in zip(cast_in, cast_out):
        dst_ref[...] = src_ref[...].astype(dst_ref.dtype)

    tb = p_ref.shape[0]
    c_sz = WKV_CHUNK
    n_quads = rw // QUAD
    qs = range(n_quads)
    lanes = [slice(q * QUAD, (q + 1) * QUAD) for q in qs]
    rows = HEADS_PER_QUAD * c_sz
    assert rows == QUAD

    @pl.when(pl.program_id(1) == 0)
    def _():
        carry_ref[...] = jnp.zeros_like(carry_ref)
        state_ref[...] = jnp.zeros_like(state_ref)

    ri = lax.broadcasted_iota(jnp.int32, (rows, QUAD), 0)
    ci = lax.broadcasted_iota(jnp.int32, (rows, QUAD), 1)
    ones_bd = jnp.where((ri // HEAD) == (ci // HEAD), 1.0, 0.0).astype(F32)
    avg_bd = ones_bd * (1.0 / HEAD)
    head_mask = (ri // c_sz) == (ci // HEAD)
    same = (ri // c_sz) == (ci // c_sz)
    strict = same & (ci < ri)
    incl = same & (ci <= ri)
    blocks = {}
    n = INV_LEAF
    while n <= c_sz:
        blocks[n] = (ri // n) == (ci // n)
        n *= 2
    eye = ri == ci
    eye_f = jnp.where(eye, 1.0, 0.0).astype(F32)
    crow = lax.broadcasted_iota(jnp.int32, (c_sz, 1), 0)

    def head_sums(xs):
        lhs = jnp.concatenate([x[:, lanes[q]] for x in xs for q in qs], axis=0)
        out = _mm(lhs, qz(ones_bd))
        return [jnp.concatenate([out[(i * n_quads + q) * c_sz:(i * n_quads + q + 1) * c_sz]
                                 for q in qs], axis=1) for i in range(len(xs))]

    def b16(x):
        return x.astype(BF16)

    def stack(x):
        return qz(jnp.where(head_mask, jnp.concatenate([b16(x)] * HEADS_PER_QUAD, axis=0), 0.0))

    half = QUAD // 2

    def block_diag(a, b):
        zero_q = jnp.zeros((half, half), a.dtype)
        return jnp.concatenate([jnp.concatenate([a, zero_q], axis=1),
                                jnp.concatenate([zero_q, b], axis=1)], axis=0)

    def qz(x):
        return block_diag(x[:half, :half], x[half:, half:])

    def expand(x):
        return jnp.concatenate([x, x], axis=1)

    lo = 3 * rw
    p_l = p_ref[:, lo:]
    brow = lax.broadcasted_iota(jnp.int32, (tb, 1), 0)
    prev_l = jnp.where(brow == 0, carry_ref[0:1, lo:], pltpu.roll(p_l, 1, 0))
    sh_l = p_l + (prev_l - p_l) * mu_ref[:, lo:]

    def lora(act, win, start, w_ref):
        before, after = start - win[0], win[1] - start - w_ref.shape[0]
        assert before % SUBLANE == 0 and after % SUBLANE == 0
        pieces = [jnp.zeros((before, rw), F32), w_ref[...], jnp.zeros((after, rw), F32)]
        w = jnp.concatenate([x for x in pieces if x.shape[0]], axis=0)
        return _mm(act(sh_l[:, win[0] - lo:win[1] - lo]), w)

    o_a = lo + wup_ref.shape[0]
    o_g = o_a + aup_ref.shape[0]
    wl_all = w0_ref[...] + lora(jnp.tanh, win_w, lo, wup_ref)
    apre_all = a0_ref[...] + lora(lambda x: x, win_a, o_a, aup_ref)
    g_all = lora(jax.nn.sigmoid, win_g, o_g, gup_ref)

    def prep_chunk(c):
        r0 = c * c_sz
        p = p_ref[r0:r0 + c_sz, :lo]
        before = carry_ref[0:1, :lo] if c == 0 else p_ref[r0 - 1:r0, :lo]
        prev = jnp.where(crow == 0, before, pltpu.roll(p, 1, 0))
        sh = p + (prev - p) * mu_ref[:, :lo]
        r = sh[:, 0:rw]
        k = sh[:, rw:2 * rw]
        v = sh[:, 2 * rw:3 * rw]

        lw = (-(2.718281828459045 ** -0.5)) * jax.nn.sigmoid(wl_all[r0:r0 + c_sz])
        a = jax.nn.sigmoid(apre_all[r0:r0 + c_sz])
        g = g_all[r0:r0 + c_sz]

        kk = k * kk_ref[...]
        k2 = k * (1.0 + (a - 1.0) * ka_ref[...])
        kk_ss, rk_sum = head_sums([kk * kk, r * k2 * rk_ref[...]])
        kk = kk * jnp.minimum(lax.rsqrt(kk_ss), 1e12)
        bonus = rk_sum * v

        cum = lw
        shift = 1
        while shift < c_sz:
            cum = cum + jnp.where(crow >= shift, pltpu.roll(cum, shift, 0), 0.0)
            shift *= 2
        ec = jnp.exp(cum)
        eci = jnp.exp(-cum)
        ec_prev = jnp.where(crow == 0, 1.0, pltpu.roll(ec, 1, 0))
        return dict(ah=-kk * ec_prev, rh=r * ec, bh=kk * a * eci, kh=k2 * eci, v=v,
                    pc=ec[c_sz - 1:c_sz, :], bonus=bonus, g=g)

    def wkv_chunk(d):
        ah = [d["ah"][:, lanes[q]] for q in qs]
        rh = [d["rh"][:, lanes[q]] for q in qs]
        bh = [d["bh"][:, lanes[q]] for q in qs]
        kh = [d["kh"][:, lanes[q]] for q in qs]
        vv = [d["v"][:, lanes[q]] for q in qs]
        pc = [d["pc"][:, lanes[q]] for q in qs]

        ah_s = [stack(x) for x in ah]
        rh_s = [stack(x) for x in rh]
        v_s = [stack(x) for x in vv]
        s = [_mm_nt(jnp.concatenate([ah_s[q], rh_s[q]], axis=0),
                    b16(jnp.concatenate([bh[q], bh[q], kh[q], kh[q]], axis=0))) for q in qs]
        a_ab = [qz(jnp.where(strict, expand(s[q][:rows, :QUAD // 2]), 0.0)) for q in qs]
        a_ak = [qz(b16(jnp.where(strict, expand(s[q][:rows, QUAD // 2:]), 0.0))) for q in qs]
        a_rb = [qz(b16(jnp.where(incl, expand(s[q][rows:, :QUAD // 2]), 0.0))) for q in qs]
        a_rk = [qz(b16(jnp.where(incl, expand(s[q][rows:, QUAD // 2:]), 0.0))) for q in qs]

        a_d = [qz(jnp.where(blocks[INV_LEAF], a_ab[q], 0.0)) for q in qs]
        t_inv = [qz(eye_f + a_d[q]) for q in qs]
        pw = [b16(a_d[q]) for q in qs]
        n = 2
        if n < INV_LEAF:
            pw = [qz(b16(_mm(pw[q], pw[q]))) for q in qs]
        while n < INV_LEAF:
            if 2 * n < INV_LEAF:
                t16 = [b16(t_inv[q]) for q in qs]
                x_a = [_mm(pw[q][:half, :half],
                           jnp.concatenate([t16[q][:half, :half], pw[q][:half, :half]], axis=1))
                       for q in qs]
                x_b = [_mm(pw[q][half:, half:],
                           jnp.concatenate([t16[q][half:, half:], pw[q][half:, half:]], axis=1))
                       for q in qs]
                t_inv = [t_inv[q] + block_diag(x_a[q][:, :half], x_b[q][:, :half]) for q in qs]
                pw = [block_diag(b16(x_a[q][:, half:]), b16(x_b[q][:, half:])) for q in qs]
            else:
                t_inv = [qz(t_inv[q] + _mm(pw[q], b16(t_inv[q]))) for q in qs]
            n *= 2
        n = INV_LEAF
        while n < c_sz:
            off = blocks[2 * n] & jnp.logical_not(blocks[n])
            a_off = [qz(b16(jnp.where(off, a_ab[q], 0.0))) for q in qs]
            t16 = [b16(t_inv[q]) for q in qs]
            ta = [qz(b16(_mm(t16[q], a_off[q]))) for q in qs]
            t_inv = [qz(t_inv[q] + _mm(ta[q], t16[q])) for q in qs]
            n *= 2

        akv = [qz(b16(_mm(a_ak[q], v_s[q]))) for q in qs]
        t16 = [b16(t_inv[q]) for q in qs]
        wu_a = [_mm(t16[q][:half, :half],
                    jnp.concatenate([ah_s[q][:half, :half], akv[q][:half, :half]], axis=1)) for q in qs]
        wu_b = [_mm(t16[q][half:, half:],
                    jnp.concatenate([ah_s[q][half:, half:], akv[q][half:, half:]], axis=1)) for q in qs]
        wt = [block_diag(b16(wu_a[q][:, :half]), b16(wu_b[q][:, :half])) for q in qs]
        ut = [block_diag(wu_a[q][:, half:], wu_b[q][:, half:]) for q in qs]
        bk = [jnp.concatenate([qz(stack(bh[q] * pc[q]).T), qz(stack(kh[q] * pc[q]).T)], axis=1)
              for q in qs]
        pc_col = [jnp.sum(jnp.where(eye, pc[q], 0.0), axis=1, keepdims=True) for q in qs]
        return dict(rh_s=rh_s, v_s=v_s, a_rb=a_rb, a_rk=a_rk, wt=wt, ut=ut, bk=bk, pc_col=pc_col)

    def wkv_state(w):
        rh_s, v_s, a_rb, a_rk = w["rh_s"], w["v_s"], w["a_rb"], w["a_rk"]
        wt, ut, bk, pc_col = w["wt"], w["ut"], w["bk"], w["pc_col"]
        m_st = [state_ref[q] for q in qs]
        m16 = [qz(b16(m_st[q])) for q in qs]
        u16 = [qz(b16(_mm(wt[q], m16[q]) + ut[q])) for q in qs]
        muv = [jnp.concatenate([m16[q], u16[q], v_s[q]], axis=0) for q in qs]
        y_s = [_mm(jnp.concatenate([rh_s[q], a_rb[q], a_rk[q]], axis=1), muv[q]) for q in qs]
        ys = []
        for q in qs:
            ys.append(jnp.concatenate(
                [y_s[q][0:c_sz, :half] + y_s[q][c_sz:2 * c_sz, :half],
                 y_s[q][2 * c_sz:3 * c_sz, half:] + y_s[q][3 * c_sz:, half:]], axis=1))
            state_ref[q] = (pc_col[q] * m_st[q]
                            + _mm(bk[q], jnp.concatenate([u16[q], v_s[q]], axis=0)))
        return ys

    def post_chunk(c, ys, d):
        y = jnp.concatenate(ys, axis=0)
        yc = y - _mm(y, qz(avg_bd))
        var = _mm(yc * yc, qz(avg_bd))
        yn = yc * lax.rsqrt(var + LNX_EPS)
        yn = jnp.concatenate([yn[q * c_sz:(q + 1) * c_sz] for q in qs], axis=1)
        out = (yn * lng_ref[...] + lnb_ref[...] + d["bonus"]) * d["g"]
        o_ref[c * c_sz:(c + 1) * c_sz, :] = out.astype(o_ref.dtype)

    n_chunks = tb // c_sz
    nxt = prep_chunk(0)
    nxt_w = wkv_chunk(nxt)
    for c in range(n_chunks):
        cur, cur_w = nxt, nxt_w
        if c + 1 < n_chunks:
            nxt = prep_chunk(c + 1)
            nxt_w = wkv_chunk(nxt)
        post_chunk(c, wkv_state(cur_w), cur)
    carry_ref[0:1, :] = p_ref[tb - 1:tb, :]


def _lane_window(start, stop):
    return (start // LANE) * LANE, -(-stop // LANE) * LANE


def _rwkv_mix(p, mu, w0, w_lora_up, a0, a_lora_up, g_lora_up, k_k, k_a, r_k, lnx_g, lnx_b, cast, *,
              tb):
    bsz, t, cols = p.shape
    rw = w0.shape[1]
    o1 = 3 * rw
    o2 = o1 + w_lora_up.shape[0]
    o3 = o2 + a_lora_up.shape[0]
    o4 = o3 + g_lora_up.shape[0]
    win_w, win_a, win_g = _lane_window(o1, o2), _lane_window(o2, o3), _lane_window(o3, o4)
    assert win_g[1] == cols
    wup, aup, gup = w_lora_up, a_lora_up, g_lora_up
    row_spec = lambda w: pl.BlockSpec((1, w), lambda b, i: (0, 0))
    full = lambda a: pl.BlockSpec(a.shape, lambda b, i: (0, 0))
    nt = t // tb
    steps = bsz * nt
    slabs = [_slab_specs(*c, steps, lambda b, i: b * nt + i) for c in cast]
    outs = pl.pallas_call(
        functools.partial(_rwkv_mix_kernel, rw=rw, win_w=win_w, win_a=win_a, win_g=win_g,
                          n_cast=len(cast)),
        grid=(bsz, nt),
        in_specs=[
            pl.BlockSpec((None, tb, cols), lambda b, i: (b, i, 0)),
            row_spec(cols), row_spec(rw), full(wup), row_spec(rw), full(aup), full(gup),
            row_spec(rw), row_spec(rw), row_spec(rw), row_spec(rw), row_spec(rw),
        ] + [sl[0] for sl in slabs],
        out_specs=[pl.BlockSpec((None, tb, rw), lambda b, i: (b, i, 0))] + [sl[1] for sl in slabs],
        out_shape=[jax.ShapeDtypeStruct((bsz, t, rw), BF16)] + [sl[2] for sl in slabs],
        scratch_shapes=[pltpu.VMEM((SUBLANE, cols), F32),
                        pltpu.VMEM((rw // QUAD, QUAD, QUAD), F32)],
        compiler_params=pltpu.CompilerParams(
            dimension_semantics=("parallel", "arbitrary"), vmem_limit_bytes=VMEM_LIMIT),
        name="rwkv_mix",
    )(p, mu, w0, wup, a0, aup, gup, k_k, k_a, r_k, lnx_g, lnx_b, *[a for a, _, _ in cast])
    return outs[0], outs[1:]


def _gelu(y):
    return 0.5 * y * (1.0 + lax.erf(y * (2.0 ** -0.5)))


def _front_kernel(x_ref, g_ref, wzt_ref, lng_ref, lnb_ref, ws_ref, bs_ref, cast_ref, n_ref, o_ref,
                  cast16_ref, w16_ref, *, sub):
    tm, width = o_ref.shape
    cast16_ref[...] = cast_ref[...].astype(BF16)

    @pl.when(pl.program_id(0) == 0)
    def _():
        w16_ref[...] = wzt_ref[...].astype(BF16)

    n = _rmsnorm(x_ref[...], g_ref[...]).astype(BF16)
    n_ref[...] = n

    def z_cols(c0):
        return _gelu(_mm_nt(n, w16_ref[c0:c0 + sub, :]))

    v = jnp.concatenate([z_cols(width + k * sub) for k in range(width // sub)], axis=1)
    mean = jnp.mean(v, axis=-1, keepdims=True)
    vc = v - mean
    var = jnp.mean(vc * vc, axis=-1, keepdims=True)
    vn = vc * lax.rsqrt(var + LN_EPS) * lng_ref[...] + lnb_ref[...]

    tr = lax.broadcasted_iota(jnp.int32, (SGU_CHUNK, SGU_CHUNK), 0)
    tc = lax.broadcasted_iota(jnp.int32, (SGU_CHUNK, SGU_CHUNK), 1)
    causal = tc <= tr
    for k in range(width // sub):
        u = z_cols(k * sub)
        for gi in range(sub // SGU_GROUP):
            g = k * (sub // SGU_GROUP) + gi
            w_c = jnp.where(causal, ws_ref[g], 0.0)
            lanes = slice(g * SGU_GROUP, (g + 1) * SGU_GROUP)
            ul = slice(gi * SGU_GROUP, (gi + 1) * SGU_GROUP)
            for c in range(tm // SGU_CHUNK):
                rows = slice(c * SGU_CHUNK, (c + 1) * SGU_CHUNK)
                mixed = _mm(w_c, vn[rows, lanes]) + bs_ref[:, lanes]
                o_ref[rows, lanes] = (u[rows, ul] * mixed).astype(o_ref.dtype)


def _front(x, g, w_t, row0, ln_g, ln_b, w_s, b_full, cast, *, tm, sub=256):
    m, d = x.shape
    width = ln_g.shape[1]
    cast_in, cast_out, cast_shape = _slab_specs(*cast, m // tm, lambda i: i)
    return pl.pallas_call(
        functools.partial(_front_kernel, sub=sub),
        grid=(m // tm,),
        in_specs=[
            pl.BlockSpec((tm, d), lambda i: (i, 0)),
            pl.BlockSpec((1, d), lambda i: (0, 0)),
            pl.BlockSpec((pl.Element(2 * width), pl.Element(d)), lambda i: (row0, 0),
                         pipeline_mode=pl.Buffered(1)),
            pl.BlockSpec((1, width), lambda i: (0, 0)),
            pl.BlockSpec((1, width), lambda i: (0, 0)),
            pl.BlockSpec(w_s.shape, lambda i: (0, 0, 0)),
            pl.BlockSpec(b_full.shape, lambda i: (0, 0)),
            cast_in,
        ],
        out_specs=[pl.BlockSpec((tm, d), lambda i: (i, 0)), pl.BlockSpec((tm, width), lambda i: (i, 0)),
                   cast_out],
        out_shape=[jax.ShapeDtypeStruct((m, d), BF16), jax.ShapeDtypeStruct((m, width), BF16),
                   cast_shape],
        scratch_shapes=[pltpu.VMEM((2 * width, d), BF16)],
        compiler_params=pltpu.CompilerParams(
            dimension_semantics=("arbitrary",), vmem_limit_bytes=VMEM_LIMIT),
        name="front",
    )(x, g, w_t, ln_g, ln_b, w_s, b_full, cast[0])


def _merge_kernel(n1_ref, ya_ref, yb_ref, x_ref, wgt_ref, wa_ref, wb_ref, wo_ref, g2_ref, h_ref,
                  n_ref, *, sub):
    d = x_ref.shape[1]
    n1 = n1_ref[...]
    ya = ya_ref[...]
    yb = yb_ref[...]
    parts = []
    for k in range(d // sub):
        cols = slice(k * sub, (k + 1) * sub)
        ga = jax.nn.sigmoid(_mm_nt(n1, wgt_ref[k * sub:(k + 1) * sub, :]))
        gb = jax.nn.sigmoid(_mm_nt(n1, wgt_ref[d + k * sub:d + (k + 1) * sub, :]))
        parts.append((ga * _mm(ya, wa_ref[:, cols]) + gb * _mm(yb, wb_ref[:, cols])).astype(BF16))
    h = x_ref[...] + _mm(jnp.concatenate(parts, axis=1), wo_ref[...])
    h_ref[...] = h
    n_ref[...] = _rmsnorm(h, g2_ref[...]).astype(BF16)


def _merge(n1, ya, yb, x, wgt, wa, wb, wo, g2, *, tm, sub=256):
    m, d = x.shape
    rw = ya.shape[1]
    const = lambda a: pl.BlockSpec(a.shape, lambda i: (0, 0), pipeline_mode=pl.Buffered(1))
    return pl.pallas_call(
        functools.partial(_merge_kernel, sub=sub),
        grid=(m // tm,),
        in_specs=[
            pl.BlockSpec((tm, d), lambda i: (i, 0)),
            pl.BlockSpec((tm, rw), lambda i: (i, 0)),
            pl.BlockSpec((tm, rw), lambda i: (i, 0)),
            pl.BlockSpec((tm, d), lambda i: (i, 0)),
            const(wgt), const(wa), const(wb), const(wo),
            pl.BlockSpec((1, d), lambda i: (0, 0)),
        ],
        out_specs=[pl.BlockSpec((tm, d), lambda i: (i, 0)), pl.BlockSpec((tm, d), lambda i: (i, 0))],
        out_shape=[jax.ShapeDtypeStruct((m, d), F32), jax.ShapeDtypeStruct((m, d), BF16)],
        compiler_params=pltpu.CompilerParams(
            dimension_semantics=("parallel",), vmem_limit_bytes=VMEM_LIMIT),
        name="merge",
    )(n1, ya, yb, x, wgt, wa, wb, wo, g2)


def _ffn_kernel(n_ref, h_ref, wg_ref, wu_ref, wd_ref, gf_ref, o_ref, *, sub):
    j = pl.program_id(1)

    @pl.when(j == 0)
    def _():
        o_ref[...] = h_ref[...]

    n = n_ref[...]
    part = None
    for k in range(wg_ref.shape[1] // sub):
        cols = slice(k * sub, (k + 1) * sub)
        gate = _mm(n, wg_ref[:, cols])
        up = _mm(n, wu_ref[:, cols])
        act = (gate * jax.nn.sigmoid(gate) * up).astype(BF16)
        down = _mm(act, wd_ref[cols, :])
        part = down if part is None else part + down
    o_ref[...] += part

    @pl.when(j == pl.num_programs(1) - 1)
    def _():
        o_ref[...] = _rmsnorm(o_ref[...], gf_ref[...])


def _ffn(n2, h1, wg, wu, wd, gf, *, tm, th, sub=256):
    m, d = h1.shape
    hid = wg.shape[1]
    return pl.pallas_call(
        functools.partial(_ffn_kernel, sub=sub),
        grid=(m // tm, hid // th),
        in_specs=[
            pl.BlockSpec((tm, d), lambda i, j: (i, 0)),
            pl.BlockSpec((tm, d), lambda i, j: (i, 0)),
            pl.BlockSpec((d, th), lambda i, j: (0, j)),
            pl.BlockSpec((d, th), lambda i, j: (0, j)),
            pl.BlockSpec((th, d), lambda i, j: (j, 0)),
            pl.BlockSpec((1, d), lambda i, j: (0, 0)),
        ],
        out_specs=pl.BlockSpec((tm, d), lambda i, j: (i, 0)),
        out_shape=jax.ShapeDtypeStruct((m, d), F32),
        compiler_params=pltpu.CompilerParams(
            dimension_semantics=("parallel", "arbitrary"), vmem_limit_bytes=VMEM_LIMIT),
        name="ffn",
    )(n2, h1, wg, wu, wd, gf)


def _pad_cols(a, width):
    return jnp.pad(a, ((0, 0), (0, width - a.shape[1])))


def _block(h, norm_mix_g, w_in, shift_mu, w0, w_lora_up, a0, a_lora_up, g_lora_up, k_k, k_a, r_k,
           lnx_g, lnx_b, w_proj_rwkv, sgu_ln_g, sgu_ln_b, sgu_w, sgu_b, w_proj_sgu, w_out,
           norm_ffn_g, w_ffn_gate, w_ffn_up, w_ffn_down, g_last, *, bsz, seq):
    m, d = h.shape
    rw = w0.shape[0]
    sw = sgu_ln_g.shape[0]
    rcols = shift_mu.shape[0]
    zc = rcols + 2 * sw
    rcols_al = -(-rcols // LANE) * LANE
    row = lambda a: a.reshape(1, -1)

    w_t = w_in.T
    mu = _pad_cols(shift_mu[None, :], rcols_al)

    b_full = jnp.repeat(sgu_b.T, SGU_GROUP, axis=1)
    n1, yb, w_rwkv16t = _front(h, row(norm_mix_g), w_t, rcols, row(sgu_ln_g), row(sgu_ln_b), sgu_w,
                               b_full, (w_t, 0, rcols_al), tm=512)
    p = _proj(n1, w_rwkv16t, tm=1024)

    whole = lambda a: (a, 0, a.shape[0])
    ya, (wgt, wa, wb, wo, wg, wu, wd) = _rwkv_mix(
        p.reshape(bsz, seq, -1), mu, row(w0), w_lora_up, row(a0), a_lora_up, g_lora_up,
        row(k_k), row(k_a), row(r_k), row(lnx_g), row(lnx_b),
        ((w_t, zc, 2 * d), whole(w_proj_rwkv), whole(w_proj_sgu), whole(w_out),
         whole(w_ffn_gate), whole(w_ffn_up), whole(w_ffn_down)), tb=256)

    h1, n2 = _merge(n1, ya.reshape(m, rw), yb, h, wgt, wa, wb, wo, row(norm_ffn_g), tm=256)
    return _ffn(n2, h1, wg, wu, wd, row(g_last), tm=1024, th=512)


def kernel(x, norm_mix_g, w_in, shift_mu, w0, w_lora_up, a0, a_lora_up, g_lora_up, k_k, k_a, r_k,
           lnx_g, lnx_b, w_proj_rwkv, sgu_ln_g, sgu_ln_b, sgu_w, sgu_b, w_proj_sgu, w_out,
           norm_ffn_g, w_ffn_gate, w_ffn_up, w_ffn_down, norm_final_g):
    bsz, seq, d = x.shape
    depth = w_in.shape[0]
    assert depth == 1, "the final rmsnorm is fused into the (single) layer's channel mixer"
    h = x.reshape(bsz * seq, d)
    out = _block(h, norm_mix_g[0], w_in[0], shift_mu[0], w0[0], w_lora_up[0], a0[0], a_lora_up[0],
                 g_lora_up[0], k_k[0], k_a[0], r_k[0], lnx_g[0], lnx_b[0], w_proj_rwkv[0],
                 sgu_ln_g[0], sgu_ln_b[0], sgu_w[0], sgu_b[0], w_proj_sgu[0], w_out[0],
                 norm_ffn_g[0], w_ffn_gate[0], w_ffn_up[0], w_ffn_down[0], norm_final_g,
                 bsz=bsz, seq=seq)
    return out.reshape(bsz, seq, d)
```
